```python
import math
import numpy as np
import jax
import jax.numpy as jnp
from jax import lax

D_MODEL = 1024
BATCH = 8
SEQ = 2048
DEPTH = 4

HEAD_DIM = 64
DIL_PAIRS = ((128, 1), (512, 4), (2048, 16))
N_DIL_GROUPS = len(DIL_PAIRS)
HEADS_PER_DIL_GROUP = 4
N_HEADS_A = N_DIL_GROUPS * HEADS_PER_DIL_GROUP
N_HEADS_B = 8
MOBA_BLOCK = 256
MOBA_TOPK = 3
MOBA_Q_CHUNK = 32
ROPE_THETA = 500000.0
ROT_DIM = HEAD_DIM // 4
D_FF = 4 * D_MODEL
CONV_WIDTH = 3
RMS_EPS = 1e-6
WIDTH_A = N_HEADS_A * HEAD_DIM
WIDTH_A_OUT = HEADS_PER_DIL_GROUP * HEAD_DIM
WIDTH_B = N_HEADS_B * HEAD_DIM
SPLIT_SIZES = (WIDTH_A, WIDTH_A, WIDTH_A, WIDTH_B, WIDTH_B, WIDTH_B, D_MODEL, D_MODEL)
D_IN = sum(SPLIT_SIZES)

kernel_name = "hybrid_dilated_moba_gated_convffn"


def rmsnorm(x, g):
    xf = x.astype(jnp.float32)
    y = xf * lax.rsqrt(jnp.mean(xf * xf, axis=-1, keepdims=True) + RMS_EPS)
    return (y * g.astype(jnp.float32)).astype(x.dtype)


def partial_rope(t, positions):
    half = ROT_DIM // 2
    inv_freq = ROPE_THETA ** (-jnp.arange(0, ROT_DIM, 2, dtype=jnp.float32) / ROT_DIM)
    ang = positions.astype(jnp.float32)[:, None, :, None] * inv_freq
    cos = jnp.cos(ang).astype(t.dtype)
    sin = jnp.sin(ang).astype(t.dtype)
    x1 = t[..., :half]
    x2 = t[..., half:ROT_DIM]
    return jnp.concatenate([x1 * cos - x2 * sin, x2 * cos + x1 * sin, t[..., ROT_DIM:]], axis=-1)


def dilated_group_attention(q, k, v, window, dilation):
    B, H, S, Dh = q.shape
    L = S // dilation
    w = window // dilation
    nb = -(-L // w)
    Lp = nb * w

    def to_sub(t):
        t = t.reshape(B, H, L, dilation, Dh).transpose(0, 1, 3, 2, 4)
        t = jnp.pad(t, ((0, 0), (0, 0), (0, 0), (0, Lp - L), (0, 0)))
        return t.reshape(B, H, dilation, nb, w, Dh)

    qb, kb, vb = to_sub(q), to_sub(k), to_sub(v)

    def with_prev(t):
        prev = jnp.concatenate([jnp.zeros_like(t[:, :, :, :1]), t[:, :, :, :-1]], axis=3)
        return jnp.concatenate([prev, t], axis=4)

    k2, v2 = with_prev(kb), with_prev(vb)
    scores = jnp.einsum('bhrnqd,bhrnkd->bhrnqk', qb, k2).astype(jnp.float32) * (Dh ** -0.5)
    qi = jnp.arange(w)[:, None]
    kj = jnp.arange(2 * w)[None, :]
    dist = w + qi - kj
    band = (dist >= 0) & (dist <= w)
    blk = jnp.arange(nb)[:, None, None]
    valid = band[None] & ((blk > 0) | (kj[None] >= w))
    scores = jnp.where(valid, scores, -jnp.inf)
    lse = jax.nn.logsumexp(scores, axis=-1)
    p = jnp.exp(scores - lse[..., None])
    out = jnp.einsum('bhrnqk,bhrnkd->bhrnqd', p.astype(v.dtype), v2)
    out = out.reshape(B, H, dilation, Lp, Dh)[:, :, :, :L].transpose(0, 1, 3, 2, 4).reshape(B, H, S, Dh)
    lse = lse.reshape(B, H, dilation, Lp)[..., :L].transpose(0, 1, 3, 2).reshape(B, H, S)
    return out, lse


def dilated_mixture_attention(qa, ka, va):
    outs, lses = [], []
    for g, (window, dilation) in enumerate(DIL_PAIRS):
        sl = slice(g * HEADS_PER_DIL_GROUP, (g + 1) * HEADS_PER_DIL_GROUP)
        o, l = dilated_group_attention(qa[:, sl], ka[:, sl], va[:, sl], window, dilation)
        outs.append(o)
        lses.append(l)
    outs = jnp.stack(outs, axis=0)
    alpha = jax.nn.softmax(jnp.stack(lses, axis=0), axis=0)
    return jnp.sum(alpha[..., None].astype(outs.dtype) * outs, axis=0)


def moba_attention(q, k, v):
    B, H, S, Dh = q.shape
    nblk = -(-S // MOBA_BLOCK)
    Sp = nblk * MOBA_BLOCK
    pad = ((0, 0), (0, 0), (0, Sp - S), (0, 0))
    q, k, v = jnp.pad(q, pad), jnp.pad(k, pad), jnp.pad(v, pad)
    kb = k.reshape(B, H, nblk, MOBA_BLOCK, Dh)
    vb = v.reshape(B, H, nblk, MOBA_BLOCK, Dh)
    k_mean = jnp.mean(kb.astype(jnp.float32), axis=3)
    n_chunks = Sp // MOBA_Q_CHUNK
    chunks_per_block = MOBA_BLOCK // MOBA_Q_CHUNK
    k_eff = min(MOBA_TOPK, nblk)
    scale = Dh ** -0.5
    qc = q.reshape(B, H, n_chunks, MOBA_Q_CHUNK, Dh).transpose(2, 0, 1, 3, 4)
    bi = jnp.arange(B)[:, None, None, None]
    hi = jnp.arange(H)[None, :, None, None]

    def one_chunk(args):
        c, q_c = args
        blk = c // chunks_per_block
        gate = jnp.einsum('bhqd,bhnd->bhqn', q_c.astype(jnp.float32), k_mean)
        past = jnp.arange(nblk) < blk
        gate = jnp.where(past, gate, -jnp.inf)
        top_val, top_idx = lax.top_k(gate, k_eff)
        sel_valid = jnp.isfinite(top_val)
        k_sel = kb[bi, hi, top_idx]
        v_sel = vb[bi, hi, top_idx].reshape(B, H, MOBA_Q_CHUNK, k_eff * MOBA_BLOCK, Dh)
        qs = q_c * scale
        s_sel = jnp.einsum('bhqd,bhqjkd->bhqjk', qs, k_sel).astype(jnp.float32)
        s_sel = jnp.where(sel_valid[..., None], s_sel, -jnp.inf).reshape(B, H, MOBA_Q_CHUNK, k_eff * MOBA_BLOCK)
        k_own = lax.dynamic_index_in_dim(kb, blk, axis=2, keepdims=False)
        v_own = lax.dynamic_index_in_dim(vb, blk, axis=2, keepdims=False)
        s_own = jnp.einsum('bhqd,bhkd->bhqk', qs, k_own).astype(jnp.float32)
        q_pos = (c % chunks_per_block) * MOBA_Q_CHUNK + jnp.arange(MOBA_Q_CHUNK)
        causal = jnp.arange(MOBA_BLOCK)[None, :] <= q_pos[:, None]
        s_own = jnp.where(causal, s_own, -jnp.inf)
        p = jax.nn.softmax(jnp.concatenate([s_sel, s_own], axis=-1), axis=-1).astype(v.dtype)
        n_sel = k_eff * MOBA_BLOCK
        return (jnp.einsum('bhqk,bhqkd->bhqd', p[..., :n_sel], v_sel)
                + jnp.einsum('bhqk,bhkd->bhqd', p[..., n_sel:], v_own))

    outs = lax.map(one_chunk, (jnp.arange(n_chunks), qc))
    return outs.transpose(1, 2, 0, 3, 4).reshape(B, H, Sp, Dh)[:, :, :S]


def conv_ffn(h, w_up, conv_w, conv_b, w_down):
    S = h.shape[1]
    u = h @ w_up
    up = jnp.pad(u, ((0, 0), (CONV_WIDTH - 1, 0), (0, 0)))
    u = sum(up[:, j:j + S, :] * conv_w[j] for j in range(CONV_WIDTH)) + conv_b
    gate, val = jnp.split(u, 2, axis=-1)
    return (jax.nn.gelu(gate, approximate=True) * val) @ w_down


def setup_inputs(seed: int = 0) -> dict:
    key = jax.random.key(seed)
    ks = jax.random.split(key, 20)
    f32 = jnp.float32

    def nrm(k, shape, fan_in):
        return jax.random.normal(k, shape, f32) * (fan_in ** -0.5)

    def gain(k):
        return 1.0 + 0.02 * jax.random.normal(k, (DEPTH, D_MODEL), f32)

    x = jax.random.normal(ks[0], (BATCH, SEQ, D_MODEL), f32)
    offsets = jax.random.randint(ks[1], (BATCH,), 0, 4096, dtype=jnp.int32)
    positions = offsets[:, None] + jnp.arange(SEQ, dtype=jnp.int32)[None, :]
    return {
        "x": x,
        "positions": positions,
        "norm_pre_mix": gain(ks[2]),
        "w_in": nrm(ks[3], (DEPTH, D_MODEL, D_IN), D_MODEL),
        "b_gate": 0.01 * jax.random.normal(ks[4], (DEPTH, 2 * D_MODEL), f32),
        "w_proj_a": nrm(ks[5], (DEPTH, WIDTH_A_OUT, D_MODEL), WIDTH_A_OUT),
        "w_proj_b": nrm(ks[6], (DEPTH, WIDTH_B, D_MODEL), WIDTH_B),
        "w_out": nrm(ks[7], (DEPTH, D_MODEL, D_MODEL), D_MODEL),
        "norm_post_mix": gain(ks[8]),
        "norm_pre_ffn": gain(ks[9]),
        "w_up": nrm(ks[10], (DEPTH, D_MODEL, 2 * D_FF), D_MODEL),
        "conv_w": nrm(ks[11], (DEPTH, CONV_WIDTH, 2 * D_FF), CONV_WIDTH),
        "conv_b": 0.01 * jax.random.normal(ks[12], (DEPTH, 2 * D_FF), f32),
        "w_down": nrm(ks[13], (DEPTH, D_FF, D_MODEL), D_FF),
        "norm_post_ffn": gain(ks[14]),
    }


def reference(x, positions, norm_pre_mix, w_in, b_gate, w_proj_a, w_proj_b, w_out, norm_post_mix,
              norm_pre_ffn, w_up, conv_w, conv_b, w_down, norm_post_ffn):
    B, S, _ = x.shape
    split_idx = np.cumsum(SPLIT_SIZES)[:-1].tolist()

    def heads(t, n_heads):
        return t.reshape(B, S, n_heads, HEAD_DIM).transpose(0, 2, 1, 3)

    def merge_heads(t):
        return t.transpose(0, 2, 1, 3).reshape(B, S, -1)

    for l in range(DEPTH):
        h = rmsnorm(x, norm_pre_mix[l])
        proj = h @ w_in[l]
        qa, ka, va, qb, kb, vb, ga, gb = jnp.split(proj, split_idx, axis=-1)
        qa = partial_rope(heads(qa, N_HEADS_A), positions)
        ka = partial_rope(heads(ka, N_HEADS_A), positions)
        qb = partial_rope(heads(qb, N_HEADS_B), positions)
        kb = partial_rope(heads(kb, N_HEADS_B), positions)
        ya = merge_heads(dilated_mixture_attention(qa, ka, heads(va, N_HEADS_A)))
        yb = merge_heads(moba_attention(qb, kb, heads(vb, N_HEADS_B)))
        gate_a = jax.nn.sigmoid(ga + b_gate[l, :D_MODEL])
        gate_b = jax.nn.sigmoid(gb + b_gate[l, D_MODEL:])
        merged = gate_a * (ya @ w_proj_a[l]) + gate_b * (yb @ w_proj_b[l])
        x = x + rmsnorm(merged @ w_out[l], norm_post_mix[l])
        h = rmsnorm(x, norm_pre_ffn[l])
        x = x + rmsnorm(conv_ffn(h, w_up[l], conv_w[l], conv_b[l], w_down[l]), norm_post_ffn[l])
    return x
```

```python
import functools

import jax
import jax.numpy as jnp
from jax import lax
from jax.experimental import pallas as pl
from jax.experimental.pallas import tpu as pltpu

F32 = jnp.float32
BF16 = jnp.bfloat16

D_MODEL = 1024
HEAD_DIM = 64
DIL_PAIRS = ((128, 1), (512, 4), (2048, 16))
BAND = 128
HEADS_PER_GROUP = 4
GROUP_W = HEADS_PER_GROUP * HEAD_DIM
WIDTH_A = 3 * GROUP_W
WIDTH_B = 512
MOBA_BLOCK = 256
MOBA_TOPK = 3
ROPE_THETA = 500000.0
ROT_DIM = HEAD_DIM // 4
ROT_HALF = ROT_DIM // 2
D_FF = 4 * D_MODEL
RMS_EPS = 1e-6
QKV_COLS = 3 * WIDTH_A + 3 * WIDTH_B
LANES = 128
NEG = -1e30

TM = 512
FF_CHUNK = 1024
HALO = 16
VMEM_LIMIT = 56 * 1024 * 1024


def _rms(x, g):
    return x * lax.rsqrt(jnp.mean(x * x, axis=-1, keepdims=True) + RMS_EPS) * g


def _dot(a, b):
    return jnp.dot(a, b, preferred_element_type=F32)


def _dot_nt(a, b):
    return lax.dot_general(a, b, (((1,), (1,)), ((), ())), preferred_element_type=F32)


def _params(n_grid):
    return pltpu.CompilerParams(dimension_semantics=("arbitrary",) * n_grid,
                                vmem_limit_bytes=VMEM_LIMIT)


def _qkv_body(x_ref, g_ref, w_ref, c_ref, s1_ref, s2_ref,
              a0_ref, a1_ref, a2_ref, b_ref, km_ref, res_ref):
    h = _rms(x_ref[...], g_ref[...]).astype(BF16)
    cos = jnp.concatenate([c_ref[...]] * 2, axis=-1)
    sin_lo = jnp.concatenate([s1_ref[...]] * 2, axis=-1)
    sin_hi = jnp.concatenate([s2_ref[...]] * 2, axis=-1)

    def proj(c0, rope):
        t = _dot(h, w_ref[:, c0:c0 + GROUP_W])
        if rope:
            t = (t * cos + pltpu.roll(t, GROUP_W - ROT_HALF, 1) * sin_lo
                 + pltpu.roll(t, ROT_HALF, 1) * sin_hi)
        return t

    a_refs = (a0_ref, a1_ref, a2_ref)
    for g, (_, dil) in enumerate(DIL_PAIRS):
        for t in range(3):
            res = proj(t * WIDTH_A + g * GROUP_W, rope=t < 2)
            if dil == 1:
                a_refs[g][t, 0, 0] = res.astype(BF16)
            else:
                res_ref[0] = res[:, :LANES]
                res_ref[1] = res[:, LANES:]
                n = TM // dil
                for r in range(dil):
                    piece = jnp.concatenate(
                        [res_ref[0, pl.ds(r, n, stride=dil), :],
                         res_ref[1, pl.ds(r, n, stride=dil), :]], axis=-1)
                    a_refs[g][t, 0, r] = piece.astype(BF16)
    for t in range(3):
        for cc in range(WIDTH_B // GROUP_W):
            lo = cc * GROUP_W
            res = proj(3 * WIDTH_A + t * WIDTH_B + lo, rope=t < 2)
            b_ref[t, :, lo:lo + GROUP_W] = res.astype(BF16)
            if t == 1:
                for blk in range(TM // MOBA_BLOCK):
                    km_ref[0, blk:blk + 1, lo:lo + GROUP_W] = jnp.mean(
                        res[blk * MOBA_BLOCK:(blk + 1) * MOBA_BLOCK], axis=0, keepdims=True)


def _qkv_call(x, g, w, cos, sin_lo, sin_hi, batch, seq):
    tokens = x.shape[0]
    tiles_per_seq = seq // TM
    tab_spec = pl.BlockSpec((TM, LANES), lambda i: (i, 0))

    def a_spec(dil):
        return pl.BlockSpec((3, 1, dil, TM // dil, GROUP_W),
                            lambda i: (0, i // tiles_per_seq, 0, i % tiles_per_seq, 0))

    out_shape = [jax.ShapeDtypeStruct((3, batch, dil, seq // dil, GROUP_W), BF16) for _, dil in DIL_PAIRS]
    out_shape += [jax.ShapeDtypeStruct((3, tokens, WIDTH_B), BF16),
                  jax.ShapeDtypeStruct((tokens // TM, TM // MOBA_BLOCK, WIDTH_B), F32)]
    out_specs = [a_spec(dil) for _, dil in DIL_PAIRS]
    out_specs += [pl.BlockSpec((3, TM, WIDTH_B), lambda i: (0, i, 0)),
                  pl.BlockSpec((1, TM // MOBA_BLOCK, WIDTH_B), lambda i: (i, 0, 0))]
    return pl.pallas_call(
        _qkv_body,
        grid=(tokens // TM,),
        in_specs=[pl.BlockSpec((TM, D_MODEL), lambda i: (i, 0)),
                  pl.BlockSpec((1, D_MODEL), lambda i: (0, 0)),
                  pl.BlockSpec((D_MODEL, QKV_COLS), lambda i: (0, 0)),
                  tab_spec, tab_spec, tab_spec],
        out_specs=out_specs,
        out_shape=out_shape,
        scratch_shapes=[pltpu.VMEM((2, TM, LANES), F32)],
        compiler_params=_params(1),
        name="qkv_proj",
    )(x, g, w, cos, sin_lo, sin_hi)


def _head_masks(rows):
    lane = lax.broadcasted_iota(jnp.int32, (rows, GROUP_W), 1)
    return [(lane >= h * HEAD_DIM) & (lane < (h + 1) * HEAD_DIM) for h in range(HEADS_PER_GROUP)]


def _dilated_body(a0_ref, a1_ref, a2_ref, ya_ref, o_scr, lse_scr, *, seq):
    hmask = _head_masks(BAND)
    row = lax.broadcasted_iota(jnp.int32, (BAND, 2 * BAND), 0)
    col = lax.broadcasted_iota(jnp.int32, (BAND, 2 * BAND), 1)
    band2 = ((col < BAND) & (col >= row)) | ((col >= BAND) & (col - BAND <= row))
    row1 = lax.broadcasted_iota(jnp.int32, (BAND, BAND), 0)
    col1 = lax.broadcasted_iota(jnp.int32, (BAND, BAND), 1)
    band1 = col1 <= row1

    for g, (a_ref, (_, dil)) in enumerate(zip((a0_ref, a1_ref, a2_ref), DIL_PAIRS)):
        nb = seq // dil // BAND
        has_prev = nb > 1

        def block(idx, carry, a_ref=a_ref, dil=dil, nb=nb, has_prev=has_prev, g=g):
            r = idx // nb
            n = idx % nb
            row0 = pl.multiple_of(n * BAND, BAND)
            q = a_ref[0, 0, r, pl.ds(row0, BAND), :]
            k = a_ref[1, 0, r, pl.ds(row0, BAND), :]
            v = a_ref[2, 0, r, pl.ds(row0, BAND), :]
            if has_prev:
                rowp = pl.multiple_of(jnp.maximum(n - 1, 0) * BAND, BAND)
                k = jnp.concatenate([a_ref[1, 0, r, pl.ds(rowp, BAND), :], k], axis=0)
                v = jnp.concatenate([a_ref[2, 0, r, pl.ds(rowp, BAND), :], v], axis=0)
                valid = band2 & (col >= jnp.where(n > 0, 0, BAND))
            else:
                valid = band1
            qf = q.astype(F32) * (HEAD_DIM ** -0.5)
            o = jnp.zeros((BAND, GROUP_W), F32)
            lse = jnp.zeros((BAND, GROUP_W), F32)
            for h in range(HEADS_PER_GROUP):
                qh = jnp.where(hmask[h], qf, 0.0).astype(BF16)
                s = jnp.where(valid, _dot_nt(qh, k), NEG)
                m = jnp.max(s, axis=-1, keepdims=True)
                p = jnp.exp(s - m)
                l = jnp.sum(p, axis=-1, keepdims=True)
                pv = _dot(p.astype(BF16), v)
                o = jnp.where(hmask[h], pv * (1.0 / l), o)
                lse = jnp.where(hmask[h], m + jnp.log(l), lse)
            start = n * (BAND * dil) + r
            for slab in range(GROUP_W // LANES):
                if dil == 1:
                    dst = pl.ds(pl.multiple_of(start, BAND), BAND)
                else:
                    dst = pl.ds(start, BAND, stride=dil)
                o_scr[g, slab, dst, :] = o[:, slab * LANES:(slab + 1) * LANES]
                lse_scr[g, slab, dst, :] = lse[:, slab * LANES:(slab + 1) * LANES]
            return carry

        lax.fori_loop(0, dil * nb, block, 0)

    chunk = 256
    for c in range(seq // chunk):
        rows = pl.ds(c * chunk, chunk)
        for slab in range(GROUP_W // LANES):
            lses = [lse_scr[g, slab, rows, :] for g in range(len(DIL_PAIRS))]
            top = jnp.maximum(jnp.maximum(lses[0], lses[1]), lses[2])
            es = [jnp.exp(l - top) for l in lses]
            num = sum(e * o_scr[g, slab, rows, :] for g, e in enumerate(es))
            ya_ref[rows, slab * LANES:(slab + 1) * LANES] = (num / (es[0] + es[1] + es[2])).astype(BF16)


def _dilated_call(a0, a1, a2, batch, seq):
    def a_spec(dil):
        return pl.BlockSpec((3, 1, dil, seq // dil, GROUP_W), lambda b: (0, b, 0, 0, 0))

    n_groups = len(DIL_PAIRS)
    return pl.pallas_call(
        functools.partial(_dilated_body, seq=seq),
        grid=(batch,),
        in_specs=[a_spec(dil) for _, dil in DIL_PAIRS],
        out_specs=pl.BlockSpec((seq, GROUP_W), lambda b: (b, 0)),
        out_shape=jax.ShapeDtypeStruct((batch * seq, GROUP_W), BF16),
        scratch_shapes=[pltpu.VMEM((n_groups, GROUP_W // LANES, seq, LANES), F32),
                        pltpu.VMEM((n_groups, GROUP_W // LANES, seq, LANES), F32)],
        compiler_params=_params(1),
        name="dilated_attn",
    )(a0, a1, a2)


def _moba_body(q_ref, k_ref, v_ref, km_ref, pen_ref, yb_ref, o_scr, *, seq):
    nblk = seq // MOBA_BLOCK
    hmask = _head_masks(MOBA_BLOCK)
    row = lax.broadcasted_iota(jnp.int32, (MOBA_BLOCK, MOBA_BLOCK), 0)
    col = lax.broadcasted_iota(jnp.int32, (MOBA_BLOCK, MOBA_BLOCK), 1)
    causal = col <= row
    km = km_ref[...]
    km_hi = km.astype(BF16)
    km_lo = (km - km_hi.astype(F32)).astype(BF16)
    blk_id = lax.broadcasted_iota(jnp.int32, (nblk, MOBA_BLOCK), 0)

    for j in range(nblk):
        own = slice(j * MOBA_BLOCK, (j + 1) * MOBA_BLOCK)
        qf = q_ref[own, :].astype(F32) * (HEAD_DIM ** -0.5)
        for h in range(HEADS_PER_GROUP):
            qh = jnp.where(hmask[h], qf, 0.0).astype(BF16)
            s = jnp.where(causal, _dot_nt(qh, k_ref[own, :]), NEG)
            if j > 0:
                past = slice(0, j * MOBA_BLOCK)
                s_past = _dot_nt(qh, k_ref[past, :])
                if j > MOBA_TOPK:
                    gate = _dot_nt(km_hi, qh) + _dot_nt(km_lo, qh)
                    beaten = jnp.zeros((nblk, MOBA_BLOCK), F32)
                    for m in range(j):
                        gm = gate[m:m + 1, :]
                        beats = (gm > gate) | ((gm == gate) & (blk_id > m))
                        beaten = beaten + jnp.where(beats, 1.0, 0.0)
                    dropped = jnp.where((beaten >= MOBA_TOPK) & (blk_id < j), 1.0, 0.0)
                    dropped = jnp.concatenate(
                        [dropped, jnp.zeros((LANES - nblk, MOBA_BLOCK), F32)], axis=0)
                    s_past = s_past + _dot(dropped.T.astype(BF16), pen_ref[:, past])
                s = jnp.concatenate([s_past, s], axis=-1)
            m_row = jnp.max(s, axis=-1, keepdims=True)
            p = jnp.exp(s - m_row)
            l = jnp.sum(p, axis=-1, keepdims=True)
            pv = _dot(p.astype(BF16), v_ref[0:(j + 1) * MOBA_BLOCK, :])
            if h == 0:
                o_scr[...] = pv * (1.0 / l)
            else:
                o_scr[...] = jnp.where(hmask[h], pv * (1.0 / l), o_scr[...])
        yb_ref[own, :] = o_scr[...].astype(BF16)


def _moba_call(bqkv, kmean, penalty, batch, seq):
    def qkv_spec(t):
        return pl.BlockSpec((None, seq, GROUP_W), lambda b, hg: (t, b, hg))

    nblk = seq // MOBA_BLOCK
    return pl.pallas_call(
        functools.partial(_moba_body, seq=seq),
        grid=(batch, WIDTH_B // GROUP_W),
        in_specs=[qkv_spec(0), qkv_spec(1), qkv_spec(2),
                  pl.BlockSpec((None, nblk, GROUP_W), lambda b, hg: (b, 0, hg)),
                  pl.BlockSpec((LANES, seq), lambda b, hg: (0, 0))],
        out_specs=pl.BlockSpec((seq, GROUP_W), lambda b, hg: (b, hg)),
        out_shape=jax.ShapeDtypeStruct((batch * seq, WIDTH_B), BF16),
        scratch_shapes=[pltpu.VMEM((MOBA_BLOCK, GROUP_W), F32)],
        compiler_params=_params(2),
        name="moba_attn",
    )(bqkv, bqkv, bqkv, kmean, penalty)


def _mix_body(x_ref, ya_ref, yb_ref, gpre_ref, wg_ref, bg_ref, wpa_ref, wpb_ref, wo_ref, gpost_ref,
              out_ref):
    x = x_ref[...]
    h = _rms(x, gpre_ref[...]).astype(BF16)
    bias = bg_ref[...]
    gate_a = jax.nn.sigmoid(_dot(h, wg_ref[:, :D_MODEL]) + bias[:, :D_MODEL])
    merged = gate_a * _dot(ya_ref[...], wpa_ref[...])
    gate_b = jax.nn.sigmoid(_dot(h, wg_ref[:, D_MODEL:]) + bias[:, D_MODEL:])
    merged = merged + gate_b * _dot(yb_ref[...], wpb_ref[...])
    out_ref[...] = x + _rms(_dot(merged.astype(BF16), wo_ref[...]), gpost_ref[...])


def _mix_call(x, ya, yb, gpre, wg, bg, wpa, wpb, wo, gpost):
    tokens = x.shape[0]

    def full(shape):
        return pl.BlockSpec(shape, lambda i: (0,) * len(shape))

    def rows(width):
        return pl.BlockSpec((TM, width), lambda i: (i, 0))

    return pl.pallas_call(
        _mix_body,
        grid=(tokens // TM,),
        in_specs=[rows(D_MODEL), rows(GROUP_W), rows(WIDTH_B), full((1, D_MODEL)),
                  full((D_MODEL, 2 * D_MODEL)), full((1, 2 * D_MODEL)),
                  full((GROUP_W, D_MODEL)), full((WIDTH_B, D_MODEL)), full((D_MODEL, D_MODEL)),
                  full((1, D_MODEL))],
        out_specs=rows(D_MODEL),
        out_shape=jax.ShapeDtypeStruct((tokens, D_MODEL), F32),
        compiler_params=_params(1),
        name="post_mixer",
    )(x, ya, yb, gpre, wg, bg, wpa, wpb, wo, gpost)


def _ffn_body(x_ref, halo_ref, gpre_ref, wug_ref, wuv_ref, cwg_ref, cwv_ref, cbg_ref, cbv_ref,
              wd_ref, gpost_ref, out_ref, h_scr, acc_scr, *, tiles_per_seq):
    i = pl.program_id(0)
    c = pl.program_id(1)

    @pl.when(c == 0)
    def _():
        g = gpre_ref[...]
        keep = jnp.where(i % tiles_per_seq == 0, 0.0, 1.0)
        h_scr[0:HALO, :] = (_rms(halo_ref[...], g) * keep).astype(BF16)
        h_scr[HALO:, :] = _rms(x_ref[...], g).astype(BF16)
        acc_scr[...] = jnp.zeros_like(acc_scr)

    h = h_scr[...]

    def conv(w_ref, cw_ref, cb_ref):
        u = _dot(h, w_ref[...])
        cw = cw_ref[...]
        return (u[HALO - 2:HALO - 2 + TM] * cw[0:1] + u[HALO - 1:HALO - 1 + TM] * cw[1:2]
                + u[HALO:] * cw[2:3] + cb_ref[...])

    gate = conv(wug_ref, cwg_ref, cbg_ref)
    val = conv(wuv_ref, cwv_ref, cbv_ref)
    act = (jax.nn.gelu(gate, approximate=True) * val).astype(BF16)
    acc_scr[...] += _dot(act, wd_ref[...])

    @pl.when(c == pl.num_programs(1) - 1)
    def _():
        out_ref[...] = x_ref[...] + _rms(acc_scr[...], gpost_ref[...])


def _ffn_call(x, gpre, wu, cw, cb, wd, gpost, seq):
    tokens = x.shape[0]
    n_chunks = D_FF // FF_CHUNK
    halo_per_tile = TM // HALO
    vec = pl.BlockSpec((1, D_MODEL), lambda i, c: (0, 0))
    return pl.pallas_call(
        functools.partial(_ffn_body, tiles_per_seq=seq // TM),
        grid=(tokens // TM, n_chunks),
        in_specs=[pl.BlockSpec((TM, D_MODEL), lambda i, c: (i, 0)),
                  pl.BlockSpec((HALO, D_MODEL), lambda i, c: (jnp.maximum(i * halo_per_tile - 1, 0), 0)),
                  vec,
                  pl.BlockSpec((D_MODEL, FF_CHUNK), lambda i, c: (0, c)),
                  pl.BlockSpec((D_MODEL, FF_CHUNK), lambda i, c: (0, n_chunks + c)),
                  pl.BlockSpec((3, FF_CHUNK), lambda i, c: (0, c)),
                  pl.BlockSpec((3, FF_CHUNK), lambda i, c: (0, n_chunks + c)),
                  pl.BlockSpec((1, FF_CHUNK), lambda i, c: (0, c)),
                  pl.BlockSpec((1, FF_CHUNK), lambda i, c: (0, n_chunks + c)),
                  pl.BlockSpec((FF_CHUNK, D_MODEL), lambda i, c: (c, 0)),
                  vec],
        out_specs=pl.BlockSpec((TM, D_MODEL), lambda i, c: (i, 0)),
        out_shape=jax.ShapeDtypeStruct((tokens, D_MODEL), F32),
        scratch_shapes=[pltpu.VMEM((HALO + TM, D_MODEL), BF16), pltpu.VMEM((TM, D_MODEL), F32)],
        compiler_params=_params(2),
        name="conv_ffn",
    )(x, x, gpre, wu, wu, cw, cw, cb, cb, wd, gpost)


def _rope_tables(positions):
    inv_freq = ROPE_THETA ** (-jnp.arange(0, ROT_DIM, 2, dtype=F32) / ROT_DIM)
    ang = positions.astype(F32).reshape(-1, 1) * inv_freq
    cos, sin = jnp.cos(ang), jnp.sin(ang)
    ones = jnp.ones((ang.shape[0], HEAD_DIM - ROT_DIM), F32)
    zeros = jnp.zeros_like(ones)
    zero_half = jnp.zeros_like(sin)
    tile = lambda t: jnp.concatenate([t] * (LANES // HEAD_DIM), axis=-1)
    return (tile(jnp.concatenate([cos, cos, ones], axis=-1)),
            tile(jnp.concatenate([-sin, zero_half, zeros], axis=-1)),
            tile(jnp.concatenate([zero_half, sin, zeros], axis=-1)))


def _block_penalty(seq):
    key_blk = jnp.arange(seq, dtype=jnp.int32) // MOBA_BLOCK
    return jnp.where(jnp.arange(LANES, dtype=jnp.int32)[:, None] == key_blk[None, :], NEG, 0.0).astype(BF16)


def kernel(x, positions, norm_pre_mix, w_in, b_gate, w_proj_a, w_proj_b, w_out, norm_post_mix,
           norm_pre_ffn, w_up, conv_w, conv_b, w_down, norm_post_ffn):
    batch, seq, _ = x.shape
    depth = w_in.shape[0]
    assert seq % TM == 0 and TM % MOBA_BLOCK == 0 and seq // DIL_PAIRS[-1][1] == BAND
    cos, sin_lo, sin_hi = _rope_tables(positions)
    penalty = _block_penalty(seq)
    w_qkv = w_in[:, :, :QKV_COLS].astype(BF16)
    w_gate = w_in[:, :, QKV_COLS:].astype(BF16)
    w_pa, w_pb, w_o = w_proj_a.astype(BF16), w_proj_b.astype(BF16), w_out.astype(BF16)
    w_u, w_d = w_up.astype(BF16), w_down.astype(BF16)
    row = lambda p: p.reshape(1, -1)

    xt = x.reshape(batch * seq, D_MODEL)
    for l in range(depth):
        a0, a1, a2, bqkv, kmean = _qkv_call(xt, row(norm_pre_mix[l]), w_qkv[l], cos, sin_lo, sin_hi,
                                            batch, seq)
        ya = _dilated_call(a0, a1, a2, batch, seq)
        yb = _moba_call(bqkv, kmean.reshape(batch, seq // MOBA_BLOCK, WIDTH_B), penalty, batch, seq)
        xt = _mix_call(xt, ya, yb, row(norm_pre_mix[l]), w_gate[l], row(b_gate[l]), w_pa[l], w_pb[l],
                       w_o[l], row(norm_post_mix[l]))
        xt = _ffn_call(xt, row(norm_pre_ffn[l]), w_u[l], conv_w[l], row(conv_b[l]), w_d[l],
                       row(norm_post_ffn[l]), seq)
    return xt.reshape(batch, seq, D_MODEL)
```

```python
import functools

import jax
import jax.numpy as jnp
from jax import lax
from jax.experimental import pallas as pl
from jax.experimental.pallas import tpu as pltpu

F32 = jnp.float32
BF16 = jnp.bfloat16

D_MODEL = 1024
HEAD_DIM = 64
DIL_PAIRS = ((128, 1), (512, 4), (2048, 16))
BAND = 128
HEADS_PER_GROUP = 4
GROUP_W = HEADS_PER_GROUP * HEAD_DIM
WIDTH_A = 3 * GROUP_W
WIDTH_B = 512
MOBA_BLOCK = 256
MOBA_TOPK = 3
ROPE_THETA = 500000.0
ROT_DIM = HEAD_DIM // 4
ROT_HALF = ROT_DIM // 2
D_FF = 4 * D_MODEL
RMS_EPS = 1e-6
QKV_COLS = 3 * WIDTH_A + 3 * WIDTH_B
LANES = 128
NEG = -1e30

TM = 512
FF_CHUNK = 1024
MOBA_LOOKAHEAD = 5
DIL_UNROLL = 4
HALO = 16
VMEM_LIMIT = 56 * 1024 * 1024


def _rms(x, g):
    return x * lax.rsqrt(jnp.mean(x * x, axis=-1, keepdims=True) + RMS_EPS) * g


def _dot(a, b):
    return jnp.dot(a, b, preferred_element_type=F32)


def _dot_nt(a, b):
    return lax.dot_general(a, b, (((1,), (1,)), ((), ())), preferred_element_type=F32)


def _params(n_grid):
    return pltpu.CompilerParams(dimension_semantics=("arbitrary",) * n_grid,
                                vmem_limit_bytes=VMEM_LIMIT)


def _qkv_body(x_ref, g_ref, w_ref, c_ref, s1_ref, s2_ref,
              a0_ref, a1_ref, a2_ref, b_ref, km_ref, res_ref):
    h = _rms(x_ref[...], g_ref[...]).astype(BF16)
    cos = jnp.concatenate([c_ref[...]] * 2, axis=-1)
    sin_lo = jnp.concatenate([s1_ref[...]] * 2, axis=-1)
    sin_hi = jnp.concatenate([s2_ref[...]] * 2, axis=-1)

    def proj(c0, rope):
        t = _dot(h, w_ref[:, c0:c0 + GROUP_W])
        if rope:
            t = (t * cos + pltpu.roll(t, GROUP_W - ROT_HALF, 1) * sin_lo
                 + pltpu.roll(t, ROT_HALF, 1) * sin_hi)
        return t

    a_refs = (a0_ref, a1_ref, a2_ref)
    for g, (_, dil) in enumerate(DIL_PAIRS):
        for t in range(3):
            res = proj(t * WIDTH_A + g * GROUP_W, rope=t < 2)
            if dil == 1:
                a_refs[g][t, 0, 0] = res.astype(BF16)
            else:
                res_ref[0] = res[:, :LANES]
                res_ref[1] = res[:, LANES:]
                n = TM // dil
                for r in range(dil):
                    piece = jnp.concatenate(
                        [res_ref[0, pl.ds(r, n, stride=dil), :],
                         res_ref[1, pl.ds(r, n, stride=dil), :]], axis=-1)
                    a_refs[g][t, 0, r] = piece.astype(BF16)
    for t in range(3):
        for cc in range(WIDTH_B // GROUP_W):
            lo = cc * GROUP_W
            res = proj(3 * WIDTH_A + t * WIDTH_B + lo, rope=t < 2)
            b_ref[t, :, lo:lo + GROUP_W] = res.astype(BF16)
            if t == 1:
                for blk in range(TM // MOBA_BLOCK):
                    km_ref[0, blk:blk + 1, lo:lo + GROUP_W] = jnp.mean(
                        res[blk * MOBA_BLOCK:(blk + 1) * MOBA_BLOCK], axis=0, keepdims=True)


def _qkv_call(layer, x, g, w, cos, sin_lo, sin_hi, batch, seq):
    tokens = x.shape[0]
    tiles_per_seq = seq // TM
    tab_spec = pl.BlockSpec((TM, LANES), lambda i: (i, 0))

    def a_spec(dil):
        return pl.BlockSpec((3, 1, dil, TM // dil, GROUP_W),
                            lambda i: (0, i // tiles_per_seq, 0, i % tiles_per_seq, 0))

    out_shape = [jax.ShapeDtypeStruct((3, batch, dil, seq // dil, GROUP_W), BF16) for _, dil in DIL_PAIRS]
    out_shape += [jax.ShapeDtypeStruct((3, tokens, WIDTH_B), BF16),
                  jax.ShapeDtypeStruct((tokens // TM, TM // MOBA_BLOCK, WIDTH_B), F32)]
    out_specs = [a_spec(dil) for _, dil in DIL_PAIRS]
    out_specs += [pl.BlockSpec((3, TM, WIDTH_B), lambda i: (0, i, 0)),
                  pl.BlockSpec((1, TM // MOBA_BLOCK, WIDTH_B), lambda i: (i, 0, 0))]
    return pl.pallas_call(
        _qkv_body,
        grid=(tokens // TM,),
        in_specs=[pl.BlockSpec((TM, D_MODEL), lambda i: (i, 0)),
                  pl.BlockSpec((None, 1, D_MODEL), lambda i: (layer, 0, 0)),
                  pl.BlockSpec((None, D_MODEL, QKV_COLS), lambda i: (layer, 0, 0)),
                  tab_spec, tab_spec, tab_spec],
        out_specs=out_specs,
        out_shape=out_shape,
        scratch_shapes=[pltpu.VMEM((2, TM, LANES), F32)],
        compiler_params=_params(1),
        name="qkv_proj",
    )(x, g, w, cos, sin_lo, sin_hi)


def _head_masks(rows):
    lane = lax.broadcasted_iota(jnp.int32, (rows, GROUP_W), 1)
    return [(lane >= h * HEAD_DIM) & (lane < (h + 1) * HEAD_DIM) for h in range(HEADS_PER_GROUP)]


def _dilated_body(a0_ref, a1_ref, a2_ref, ya_ref, o_scr, lse_scr, qmask_scr, bias_scr, *s_slots, seq):
    hmask = _head_masks(BAND)
    row = lax.broadcasted_iota(jnp.int32, (BAND, 2 * BAND), 0)
    col = lax.broadcasted_iota(jnp.int32, (BAND, 2 * BAND), 1)
    band = ((col < BAND) & (col >= row)) | ((col >= BAND) & (col - BAND <= row))
    bias_scr[1] = jnp.where(band, 0.0, NEG)
    bias_scr[0] = jnp.where(band & (col >= BAND), 0.0, NEG)
    for h in range(HEADS_PER_GROUP):
        qmask_scr[h] = jnp.where(hmask[h], HEAD_DIM ** -0.5, 0.0).astype(BF16)

    for g, (a_ref, (_, dil)) in enumerate(zip((a0_ref, a1_ref, a2_ref), DIL_PAIRS)):
        nb = seq // dil // BAND
        has_prev = nb > 1

        nk = 2 * BAND if has_prev else BAND

        def key_rows(idx, nb=nb):
            r, n = idx // nb, idx % nb
            return r, n, pl.multiple_of(n * BAND, BAND), pl.multiple_of(jnp.maximum(n - 1, 0) * BAND, BAND)

        def gather(a_ref, t, r, row0, rowp, has_prev=has_prev):
            cur = a_ref[t, 0, r, pl.ds(row0, BAND), :]
            if has_prev:
                return jnp.concatenate([a_ref[t, 0, r, pl.ds(rowp, BAND), :], cur], axis=0)
            return cur

        def blocks(it, carry, a_ref=a_ref, dil=dil, has_prev=has_prev, g=g, nk=nk):
            for u in range(DIL_UNROLL):
                r, n, row0, rowp = key_rows(it * DIL_UNROLL + u)
                q = a_ref[0, 0, r, pl.ds(row0, BAND), :]
                q4 = jnp.concatenate([q * qmask_scr[h] for h in range(HEADS_PER_GROUP)], axis=0)
                bias = bias_scr[jnp.minimum(n, 1)] if has_prev else bias_scr[1, :, BAND:]
                s = _dot_nt(q4, gather(a_ref, 1, r, row0, rowp)).reshape(HEADS_PER_GROUP, BAND, nk)
                s_slots[u][:, 0:nk] = (s + bias[None]).reshape(HEADS_PER_GROUP * BAND, nk)
            for u in range(DIL_UNROLL):
                r, n, row0, rowp = key_rows(it * DIL_UNROLL + u)
                s = s_slots[u][:, 0:nk]
                m = jnp.max(s, axis=-1, keepdims=True)
                p = jnp.exp(s - m)
                l = jnp.sum(p, axis=-1, keepdims=True)
                o4 = _dot(p.astype(BF16), gather(a_ref, 2, r, row0, rowp)) * (1.0 / l)
                lse4 = jnp.broadcast_to(m + jnp.log(l), (HEADS_PER_GROUP * BAND, GROUP_W))
                o = o4[0:BAND]
                lse = lse4[0:BAND]
                for h in range(1, HEADS_PER_GROUP):
                    o = jnp.where(hmask[h], o4[h * BAND:(h + 1) * BAND], o)
                    lse = jnp.where(hmask[h], lse4[h * BAND:(h + 1) * BAND], lse)
                start = n * (BAND * dil) + r
                for slab in range(GROUP_W // LANES):
                    if dil == 1:
                        dst = pl.ds(pl.multiple_of(start, BAND), BAND)
                    else:
                        dst = pl.ds(start, BAND, stride=dil)
                    o_scr[g, slab, dst, :] = o[:, slab * LANES:(slab + 1) * LANES]
                    lse_scr[g, slab, dst, :] = lse[:, slab * LANES:(slab + 1) * LANES]
            return carry

        lax.fori_loop(0, dil * nb // DIL_UNROLL, blocks, 0)

    chunk = 256
    for c in range(seq // chunk):
        rows = pl.ds(c * chunk, chunk)
        for slab in range(GROUP_W // LANES):
            lses = [lse_scr[g, slab, rows, :] for g in range(len(DIL_PAIRS))]
            top = jnp.maximum(jnp.maximum(lses[0], lses[1]), lses[2])
            es = [jnp.exp(l - top) for l in lses]
            num = sum(e * o_scr[g, slab, rows, :] for g, e in enumerate(es))
            ya_ref[rows, slab * LANES:(slab + 1) * LANES] = (num / (es[0] + es[1] + es[2])).astype(BF16)


def _dilated_call(a0, a1, a2, batch, seq):
    def a_spec(dil):
        return pl.BlockSpec((3, 1, dil, seq // dil, GROUP_W), lambda b: (0, b, 0, 0, 0))

    n_groups = len(DIL_PAIRS)
    return pl.pallas_call(
        functools.partial(_dilated_body, seq=seq),
        grid=(batch,),
        in_specs=[a_spec(dil) for _, dil in DIL_PAIRS],
        out_specs=pl.BlockSpec((seq, GROUP_W), lambda b: (b, 0)),
        out_shape=jax.ShapeDtypeStruct((batch * seq, GROUP_W), BF16),
        scratch_shapes=[pltpu.VMEM((n_groups, GROUP_W // LANES, seq, LANES), F32),
                        pltpu.VMEM((n_groups, GROUP_W // LANES, seq, LANES), F32),
                        pltpu.VMEM((HEADS_PER_GROUP, BAND, GROUP_W), BF16),
                        pltpu.VMEM((2, BAND, 2 * BAND), F32)]
                       + [pltpu.VMEM((HEADS_PER_GROUP * BAND, 2 * BAND), F32)] * DIL_UNROLL,
        compiler_params=_params(1),
        name="dilated_attn",
    )(a0, a1, a2)


def _fold_rows(tiles, op):
    parts = []
    for t in tiles:
        while t.shape[0] > 8:
            half = t.shape[0] // 2
            t = op(t[:half], t[half:])
        parts.append(t)
    while len(parts) > 1:
        parts = [op(parts[i], parts[i + 1]) if i + 1 < len(parts) else parts[i]
                 for i in range(0, len(parts), 2)]
    return parts[0]


def _moba_body(q_ref, k_ref, v_ref, km_ref, yb_ref, vt_scr, qmask_scr, *s_slots, seq):
    nblk = seq // MOBA_BLOCK
    hmask = _head_masks(MOBA_BLOCK)
    key = lax.broadcasted_iota(jnp.int32, (MOBA_BLOCK, MOBA_BLOCK), 0)
    qry = lax.broadcasted_iota(jnp.int32, (MOBA_BLOCK, MOBA_BLOCK), 1)
    causal = key <= qry
    km = km_ref[...]
    km_hi = km.astype(BF16)
    km_lo = (km - km_hi.astype(F32)).astype(BF16)
    blk_id = lax.broadcasted_iota(jnp.int32, (nblk, MOBA_BLOCK), 0)
    for h in range(HEADS_PER_GROUP):
        qmask_scr[h] = jnp.where(hmask[h], HEAD_DIM ** -0.5, 0.0).astype(BF16)
    for j in range(nblk):
        blk = slice(j * MOBA_BLOCK, (j + 1) * MOBA_BLOCK)
        vt_scr[:, blk] = v_ref[blk, :].T

    heads = range(HEADS_PER_GROUP)
    units = [(j, n, h) for j in range(nblk) for n in [j] + list(range(j)) for h in heads]
    qh, pen = {}, {}

    def scores(unit):
        j, n, h = unit
        if (j, h) not in qh:
            qh[j, h] = q_ref[j * MOBA_BLOCK:(j + 1) * MOBA_BLOCK, :] * qmask_scr[h]
        return _dot_nt(k_ref[n * MOBA_BLOCK:(n + 1) * MOBA_BLOCK, :], qh[j, h])

    def penalty(j, h):
        if (j, h) not in pen:
            gate = _dot_nt(km_hi, qh[j, h]) + _dot_nt(km_lo, qh[j, h])
            beaten = jnp.zeros((nblk, MOBA_BLOCK), F32)
            for m in range(j):
                gm = gate[m:m + 1, :]
                beats = (gm > gate) | ((gm == gate) & (blk_id > m))
                beaten = beaten + jnp.where(beats, 1.0, 0.0)
            pen[j, h] = jnp.where(beaten >= MOBA_TOPK, NEG, 0.0)
        return pen[j, h]

    nslot = len(s_slots)
    for i in range(min(MOBA_LOOKAHEAD, len(units))):
        s_slots[i % nslot][...] = scores(units[i])
    m_run, l_run, acc = {}, {}, {}
    for i, (j, n, h) in enumerate(units):
        if i + MOBA_LOOKAHEAD < len(units):
            s_slots[(i + MOBA_LOOKAHEAD) % nslot][...] = scores(units[i + MOBA_LOOKAHEAD])
        s = s_slots[i % nslot][...]
        if n == j:
            s = jnp.where(causal, s, NEG)
        elif j > MOBA_TOPK:
            s = s + penalty(j, h)[n:n + 1, :]
        m_blk = jnp.max(_fold_rows([s], jnp.maximum), axis=0, keepdims=True)
        m_new = m_blk if n == j else jnp.maximum(m_run[h], m_blk)
        p = jnp.exp(s - m_new)
        l_blk = jnp.sum(_fold_rows([p], jnp.add), axis=0, keepdims=True)
        vt = vt_scr[h * HEAD_DIM:(h + 1) * HEAD_DIM, n * MOBA_BLOCK:(n + 1) * MOBA_BLOCK]
        pv = _dot(vt, p.astype(BF16))
        if n == j:
            l_run[h], acc[h] = l_blk, pv
        else:
            alpha = jnp.exp(m_run[h] - m_new)
            l_run[h] = alpha * l_run[h] + l_blk
            acc[h] = alpha * acc[h] + pv
        m_run[h] = m_new
        if n == max(j - 1, 0) and h == HEADS_PER_GROUP - 1:
            out_t = jnp.concatenate([acc[hh] * (1.0 / l_run[hh]) for hh in heads], axis=0)
            yb_ref[j * MOBA_BLOCK:(j + 1) * MOBA_BLOCK, :] = out_t.T.astype(BF16)


def _moba_call(bqkv, kmean, batch, seq):
    def qkv_spec(t):
        return pl.BlockSpec((None, seq, GROUP_W), lambda b, hg: (t, b, hg))

    nblk = seq // MOBA_BLOCK
    return pl.pallas_call(
        functools.partial(_moba_body, seq=seq),
        grid=(batch, WIDTH_B // GROUP_W),
        in_specs=[qkv_spec(0), qkv_spec(1), qkv_spec(2),
                  pl.BlockSpec((None, nblk, GROUP_W), lambda b, hg: (b, 0, hg))],
        out_specs=pl.BlockSpec((seq, GROUP_W), lambda b, hg: (b, hg)),
        out_shape=jax.ShapeDtypeStruct((batch * seq, WIDTH_B), BF16),
        scratch_shapes=[pltpu.VMEM((GROUP_W, seq), BF16),
                        pltpu.VMEM((HEADS_PER_GROUP, MOBA_BLOCK, GROUP_W), BF16)]
                       + [pltpu.VMEM((MOBA_BLOCK, MOBA_BLOCK), F32)] * (MOBA_LOOKAHEAD + 1),
        compiler_params=_params(2),
        name="moba_attn",
    )(bqkv, bqkv, bqkv, kmean)


def _mix_body(x_ref, ya_ref, yb_ref, gpre_ref, wg_ref, bg_ref, wpa_ref, wpb_ref, wo_ref, gpost_ref,
              out_ref):
    x = x_ref[...]
    h = _rms(x, gpre_ref[...]).astype(BF16)
    bias = bg_ref[...]
    gate_a = jax.nn.sigmoid(_dot(h, wg_ref[:, :D_MODEL]) + bias[:, :D_MODEL])
    merged = gate_a * _dot(ya_ref[...], wpa_ref[...])
    gate_b = jax.nn.sigmoid(_dot(h, wg_ref[:, D_MODEL:]) + bias[:, D_MODEL:])
    merged = merged + gate_b * _dot(yb_ref[...], wpb_ref[...])
    out_ref[...] = x + _rms(_dot(merged.astype(BF16), wo_ref[...]), gpost_ref[...])


def _mix_call(layer, x, ya, yb, gpre, wg, bg, wpa, wpb, wo, gpost):
    tokens = x.shape[0]

    def full(shape):
        return pl.BlockSpec((None,) + shape, lambda i: (layer,) + (0,) * len(shape))

    def rows(width):
        return pl.BlockSpec((TM, width), lambda i: (i, 0))

    return pl.pallas_call(
        _mix_body,
        grid=(tokens // TM,),
        in_specs=[rows(D_MODEL), rows(GROUP_W), rows(WIDTH_B), full((1, D_MODEL)),
                  full((D_MODEL, 2 * D_MODEL)), full((1, 2 * D_MODEL)),
                  full((GROUP_W, D_MODEL)), full((WIDTH_B, D_MODEL)), full((D_MODEL, D_MODEL)),
                  full((1, D_MODEL))],
        out_specs=rows(D_MODEL),
        out_shape=jax.ShapeDtypeStruct((tokens, D_MODEL), F32),
        compiler_params=_params(1),
        name="post_mixer",
    )(x, ya, yb, gpre, wg, bg, wpa, wpb, wo, gpost)


def _ffn_body(x_ref, halo_ref, gpre_ref, wug_ref, wuv_ref, cwg_ref, cwv_ref, cbg_ref, cbv_ref,
              wd_ref, gpost_ref, out_ref, h_scr, acc_scr, *, tiles_per_seq):
    i = pl.program_id(0)
    c = pl.program_id(1)

    @pl.when(c == 0)
    def _():
        g = gpre_ref[...]
        keep = jnp.where(i % tiles_per_seq == 0, 0.0, 1.0)
        h_scr[0:HALO, :] = (_rms(halo_ref[...], g) * keep).astype(BF16)
        h_scr[HALO:, :] = _rms(x_ref[...], g).astype(BF16)
        acc_scr[...] = jnp.zeros_like(acc_scr)

    h = h_scr[...]

    def conv(w_ref, cw_ref, cb_ref):
        u = _dot(h, w_ref[...])
        cw = cw_ref[...]
        return (u[HALO - 2:HALO - 2 + TM] * cw[0:1] + u[HALO - 1:HALO - 1 + TM] * cw[1:2]
                + u[HALO:] * cw[2:3] + cb_ref[...])

    gate = conv(wug_ref, cwg_ref, cbg_ref)
    val = conv(wuv_ref, cwv_ref, cbv_ref)
    act = (jax.nn.gelu(gate, approximate=True) * val).astype(BF16)
    acc_scr[...] += _dot(act, wd_ref[...])

    @pl.when(c == pl.num_programs(1) - 1)
    def _():
        out_ref[...] = x_ref[...] + _rms(acc_scr[...], gpost_ref[...])


def _ffn_call(layer, x, gpre, wu, cw, cb, wd, gpost, seq):
    tokens = x.shape[0]
    n_chunks = D_FF // FF_CHUNK
    halo_per_tile = TM // HALO
    vec = pl.BlockSpec((None, 1, D_MODEL), lambda i, c: (layer, 0, 0))
    return pl.pallas_call(
        functools.partial(_ffn_body, tiles_per_seq=seq // TM),
        grid=(tokens // TM, n_chunks),
        in_specs=[pl.BlockSpec((TM, D_MODEL), lambda i, c: (i, 0)),
                  pl.BlockSpec((HALO, D_MODEL), lambda i, c: (jnp.maximum(i * halo_per_tile - 1, 0), 0)),
                  vec,
                  pl.BlockSpec((None, D_MODEL, FF_CHUNK), lambda i, c: (layer, 0, c)),
                  pl.BlockSpec((None, D_MODEL, FF_CHUNK), lambda i, c: (layer, 0, n_chunks + c)),
                  pl.BlockSpec((None, 3, FF_CHUNK), lambda i, c: (layer, 0, c)),
                  pl.BlockSpec((None, 3, FF_CHUNK), lambda i, c: (layer, 0, n_chunks + c)),
                  pl.BlockSpec((None, 1, FF_CHUNK), lambda i, c: (layer, 0, c)),
                  pl.BlockSpec((None, 1, FF_CHUNK), lambda i, c: (layer, 0, n_chunks + c)),
                  pl.BlockSpec((None, FF_CHUNK, D_MODEL), lambda i, c: (layer, c, 0)),
                  vec],
        out_specs=pl.BlockSpec((TM, D_MODEL), lambda i, c: (i, 0)),
        out_shape=jax.ShapeDtypeStruct((tokens, D_MODEL), F32),
        scratch_shapes=[pltpu.VMEM((HALO + TM, D_MODEL), BF16), pltpu.VMEM((TM, D_MODEL), F32)],
        compiler_params=_params(2),
        name="conv_ffn",
    )(x, x, gpre, wu, wu, cw, cw, cb, cb, wd, gpost)


def _rope_tables(positions):
    inv_freq = ROPE_THETA ** (-jnp.arange(0, ROT_DIM, 2, dtype=F32) / ROT_DIM)
    lane = jnp.arange(LANES, dtype=jnp.int32) % HEAD_DIM
    ang = positions.astype(F32).reshape(-1, 1) * inv_freq[lane % ROT_HALF][None, :]
    cos = jnp.where(lane < ROT_DIM, jnp.cos(ang), 1.0)
    sin = jnp.sin(ang)
    return (cos, jnp.where(lane < ROT_HALF, -sin, 0.0),
            jnp.where((lane >= ROT_HALF) & (lane < ROT_DIM), sin, 0.0))


def kernel(x, positions, norm_pre_mix, w_in, b_gate, w_proj_a, w_proj_b, w_out, norm_post_mix,
           norm_pre_ffn, w_up, conv_w, conv_b, w_down, norm_post_ffn):
    batch, seq, _ = x.shape
    depth = w_in.shape[0]
    assert seq % TM == 0 and TM % MOBA_BLOCK == 0 and seq // DIL_PAIRS[-1][1] == BAND
    cos, sin_lo, sin_hi = _rope_tables(positions)
    w_qkv = w_in[:, :, :QKV_COLS].astype(BF16)
    w_gate = w_in[:, :, QKV_COLS:].astype(BF16)
    w_pa, w_pb, w_o = w_proj_a.astype(BF16), w_proj_b.astype(BF16), w_out.astype(BF16)
    w_u, w_d = w_up.astype(BF16), w_down.astype(BF16)
    rows = lambda p: p.reshape(depth, 1, -1)
    g_pre_mix, g_post_mix = rows(norm_pre_mix), rows(norm_post_mix)
    g_pre_ffn, g_post_ffn = rows(norm_pre_ffn), rows(norm_post_ffn)
    bias_gate, bias_conv = rows(b_gate), rows(conv_b)

    xt = x.reshape(batch * seq, D_MODEL)
    for l in range(depth):
        a0, a1, a2, bqkv, kmean = _qkv_call(l, xt, g_pre_mix, w_qkv, cos, sin_lo, sin_hi, batch, seq)
        ya = _dilated_call(a0, a1, a2, batch, seq)
        yb = _moba_call(bqkv, kmean.reshape(batch, seq // MOBA_BLOCK, WIDTH_B), batch, seq)
        xt = _mix_call(l, xt, ya, yb, g_pre_mix, w_gate, bias_gate, w_pa, w_pb, w_o, g_post_mix)
        xt = _ffn_call(l, xt, g_pre_ffn, w_u, conv_w, bias_conv, w_d, g_post_ffn, seq)
    return xt.reshape(batch, seq, D_MODEL)
```

```python
import functools

import jax
import jax.numpy as jnp
from jax import lax
from jax.experimental import pallas as pl
from jax.experimental.pallas import tpu as pltpu

F32 = jnp.float32
BF16 = jnp.bfloat16

D_MODEL = 1024
HEAD_DIM = 64
DIL_PAIRS = ((128, 1), (512, 4), (2048, 16))
BAND = 128
HEADS_PER_GROUP = 4
GROUP_W = HEADS_PER_GROUP * HEAD_DIM
WIDTH_A = 3 * GROUP_W
WIDTH_B = 512
MOBA_BLOCK = 256
MOBA_TOPK = 3
ROPE_THETA = 500000.0
ROT_DIM = HEAD_DIM // 4
ROT_HALF = ROT_DIM // 2
D_FF = 4 * D_MODEL
RMS_EPS = 1e-6
QKV_COLS = 3 * WIDTH_A + 3 * WIDTH_B
D_IN = QKV_COLS + 2 * D_MODEL
LANES = 128
NEG = -1e30

TM = 512
TM_FFN = 1024
SUBLANES = 8
FFN_RUN = TM_FFN // SUBLANES
FF_CHUNK = 1024
FF_SUB = 256
MOBA_LOOKAHEAD = 5
VT_ROWS = HEAD_DIM + 16
DIL_UNROLL = 4
HALO = 16
VMEM_LIMIT = 56 * 1024 * 1024


def _rms(x, g):
    return x * lax.rsqrt(jnp.mean(x * x, axis=-1, keepdims=True) + RMS_EPS) * g


def _dot(a, b):
    return jnp.dot(a, b, preferred_element_type=F32)


def _dot_nt(a, b):
    return lax.dot_general(a, b, (((1,), (1,)), ((), ())), preferred_element_type=F32)


def _params(n_grid):
    return pltpu.CompilerParams(dimension_semantics=("arbitrary",) * n_grid,
                                vmem_limit_bytes=VMEM_LIMIT)


def _qkv_body(x_ref, g_ref, w_ref, c_ref, s1_ref, s2_ref,
              a0_ref, a1_ref, a2_ref, b_ref, km_ref, res_ref):
    h = _rms(x_ref[...], g_ref[...]).astype(BF16)
    cos = jnp.concatenate([c_ref[...]] * 2, axis=-1)
    sin_lo = jnp.concatenate([s1_ref[...]] * 2, axis=-1)
    sin_hi = jnp.concatenate([s2_ref[...]] * 2, axis=-1)

    def proj(c0, rope):
        t = _dot(h, w_ref[:, c0:c0 + GROUP_W])
        if rope:
            t = (t * cos + pltpu.roll(t, GROUP_W - ROT_HALF, 1) * sin_lo
                 + pltpu.roll(t, ROT_HALF, 1) * sin_hi)
        return t

    a_refs = (a0_ref, a1_ref, a2_ref)
    for g, (_, dil) in enumerate(DIL_PAIRS):
        for t in range(3):
            res = proj(t * WIDTH_A + g * GROUP_W, rope=t < 2)
            if dil == 1:
                a_refs[g][t, 0, 0] = res.astype(BF16)
            else:
                res_ref[0] = res[:, :LANES]
                res_ref[1] = res[:, LANES:]
                n = TM // dil
                for r in range(dil):
                    piece = jnp.concatenate(
                        [res_ref[0, pl.ds(r, n, stride=dil), :],
                         res_ref[1, pl.ds(r, n, stride=dil), :]], axis=-1)
                    a_refs[g][t, 0, r] = piece.astype(BF16)
    for t in range(3):
        for cc in range(WIDTH_B // GROUP_W):
            lo = cc * GROUP_W
            res = proj(3 * WIDTH_A + t * WIDTH_B + lo, rope=t < 2)
            b_ref[t, :, lo:lo + GROUP_W] = res.astype(BF16)
            if t == 1:
                for blk in range(TM // MOBA_BLOCK):
                    km_ref[0, blk:blk + 1, lo:lo + GROUP_W] = jnp.mean(
                        res[blk * MOBA_BLOCK:(blk + 1) * MOBA_BLOCK], axis=0, keepdims=True)


def _qkv_call(layer, x, g, w, cos, sin_lo, sin_hi, batch, seq):
    tokens = x.shape[0]
    tiles_per_seq = seq // TM
    tab_spec = pl.BlockSpec((TM, LANES), lambda i: (i, 0))

    def a_spec(dil):
        return pl.BlockSpec((3, 1, dil, TM // dil, GROUP_W),
                            lambda i: (0, i // tiles_per_seq, 0, i % tiles_per_seq, 0))

    out_shape = [jax.ShapeDtypeStruct((3, batch, dil, seq // dil, GROUP_W), BF16) for _, dil in DIL_PAIRS]
    out_shape += [jax.ShapeDtypeStruct((3, tokens, WIDTH_B), BF16),
                  jax.ShapeDtypeStruct((tokens // TM, TM // MOBA_BLOCK, WIDTH_B), F32)]
    out_specs = [a_spec(dil) for _, dil in DIL_PAIRS]
    out_specs += [pl.BlockSpec((3, TM, WIDTH_B), lambda i: (0, i, 0)),
                  pl.BlockSpec((1, TM // MOBA_BLOCK, WIDTH_B), lambda i: (i, 0, 0))]
    return pl.pallas_call(
        _qkv_body,
        grid=(tokens // TM,),
        in_specs=[pl.BlockSpec((TM, D_MODEL), lambda i: (i, 0)),
                  pl.BlockSpec((None, 1, D_MODEL), lambda i: (layer, 0, 0)),
                  pl.BlockSpec((None, D_MODEL, QKV_COLS), lambda i: (layer, 0, 0)),
                  tab_spec, tab_spec, tab_spec],
        out_specs=out_specs,
        out_shape=out_shape,
        scratch_shapes=[pltpu.VMEM((2, TM, LANES), F32)],
        compiler_params=_params(1),
        name="qkv_proj",
    )(x, g, w, cos, sin_lo, sin_hi)


def _head_masks(rows):
    lane = lax.broadcasted_iota(jnp.int32, (rows, GROUP_W), 1)
    return [(lane >= h * HEAD_DIM) & (lane < (h + 1) * HEAD_DIM) for h in range(HEADS_PER_GROUP)]


def _dilated_body(a0_ref, a1_ref, a2_ref, ya_ref, o_scr, lse_scr, qmask_scr, bias_scr, *s_slots, seq):
    hmask = _head_masks(BAND)
    row = lax.broadcasted_iota(jnp.int32, (BAND, 2 * BAND), 0)
    col = lax.broadcasted_iota(jnp.int32, (BAND, 2 * BAND), 1)
    band = ((col < BAND) & (col >= row)) | ((col >= BAND) & (col - BAND <= row))
    bias_scr[1] = jnp.where(band, 0.0, NEG)
    bias_scr[0] = jnp.where(band & (col >= BAND), 0.0, NEG)
    for h in range(HEADS_PER_GROUP):
        qmask_scr[h] = jnp.where(hmask[h], HEAD_DIM ** -0.5, 0.0).astype(BF16)

    for g, (a_ref, (_, dil)) in enumerate(zip((a0_ref, a1_ref, a2_ref), DIL_PAIRS)):
        nb = seq // dil // BAND
        has_prev = nb > 1

        nk = 2 * BAND if has_prev else BAND

        def key_rows(idx, nb=nb):
            r, n = idx // nb, idx % nb
            return r, n, pl.multiple_of(n * BAND, BAND), pl.multiple_of(jnp.maximum(n - 1, 0) * BAND, BAND)

        def gather(a_ref, t, r, row0, rowp, has_prev=has_prev):
            cur = a_ref[t, 0, r, pl.ds(row0, BAND), :]
            if has_prev:
                return jnp.concatenate([a_ref[t, 0, r, pl.ds(rowp, BAND), :], cur], axis=0)
            return cur

        def blocks(it, carry, a_ref=a_ref, dil=dil, has_prev=has_prev, g=g, nk=nk):
            for u in range(DIL_UNROLL):
                r, n, row0, rowp = key_rows(it * DIL_UNROLL + u)
                q = a_ref[0, 0, r, pl.ds(row0, BAND), :]
                q4 = jnp.concatenate([q * qmask_scr[h] for h in range(HEADS_PER_GROUP)], axis=0)
                bias = bias_scr[jnp.minimum(n, 1)] if has_prev else bias_scr[1, :, BAND:]
                s = _dot_nt(q4, gather(a_ref, 1, r, row0, rowp)).reshape(HEADS_PER_GROUP, BAND, nk)
                s_slots[u][:, 0:nk] = (s + bias[None]).reshape(HEADS_PER_GROUP * BAND, nk)
            for u in range(DIL_UNROLL):
                r, n, row0, rowp = key_rows(it * DIL_UNROLL + u)
                s = s_slots[u][:, 0:nk]
                m = jnp.max(s, axis=-1, keepdims=True)
                p = jnp.exp(s - m)
                l = jnp.sum(p, axis=-1, keepdims=True)
                o4 = _dot(p.astype(BF16), gather(a_ref, 2, r, row0, rowp)) * (1.0 / l)
                lse4 = jnp.broadcast_to(m + jnp.log(l), (HEADS_PER_GROUP * BAND, GROUP_W))
                o = o4[0:BAND]
                lse = lse4[0:BAND]
                for h in range(1, HEADS_PER_GROUP):
                    o = jnp.where(hmask[h], o4[h * BAND:(h + 1) * BAND], o)
                    lse = jnp.where(hmask[h], lse4[h * BAND:(h + 1) * BAND], lse)
                start = n * (BAND * dil) + r
                for slab in range(GROUP_W // LANES):
                    if dil == 1:
                        dst = pl.ds(pl.multiple_of(start, BAND), BAND)
                    else:
                        dst = pl.ds(start, BAND, stride=dil)
                    o_scr[g, slab, dst, :] = o[:, slab * LANES:(slab + 1) * LANES]
                    lse_scr[g, slab, dst, :] = lse[:, slab * LANES:(slab + 1) * LANES]
            return carry

        lax.fori_loop(0, dil * nb // DIL_UNROLL, blocks, 0)

    chunk = 256
    for c in range(seq // chunk):
        rows = pl.ds(c * chunk, chunk)
        for slab in range(GROUP_W // LANES):
            lses = [lse_scr[g, slab, rows, :] for g in range(len(DIL_PAIRS))]
            top = jnp.maximum(jnp.maximum(lses[0], lses[1]), lses[2])
            es = [jnp.exp(l - top) for l in lses]
            num = sum(e * o_scr[g, slab, rows, :] for g, e in enumerate(es))
            ya_ref[rows, slab * LANES:(slab + 1) * LANES] = (num / (es[0] + es[1] + es[2])).astype(BF16)


def _dilated_call(a0, a1, a2, batch, seq):
    def a_spec(dil):
        return pl.BlockSpec((3, 1, dil, seq // dil, GROUP_W), lambda b: (0, b, 0, 0, 0))

    n_groups = len(DIL_PAIRS)
    return pl.pallas_call(
        functools.partial(_dilated_body, seq=seq),
        grid=(batch,),
        in_specs=[a_spec(dil) for _, dil in DIL_PAIRS],
        out_specs=pl.BlockSpec((seq, GROUP_W), lambda b: (b, 0)),
        out_shape=jax.ShapeDtypeStruct((batch * seq, GROUP_W), BF16),
        scratch_shapes=[pltpu.VMEM((n_groups, GROUP_W // LANES, seq, LANES), F32),
                        pltpu.VMEM((n_groups, GROUP_W // LANES, seq, LANES), F32),
                        pltpu.VMEM((HEADS_PER_GROUP, BAND, GROUP_W), BF16),
                        pltpu.VMEM((2, BAND, 2 * BAND), F32)]
                       + [pltpu.VMEM((HEADS_PER_GROUP * BAND, 2 * BAND), F32)] * DIL_UNROLL,
        compiler_params=_params(1),
        name="dilated_attn",
    )(a0, a1, a2)


def _fold_rows(tiles, op):
    parts = []
    for t in tiles:
        while t.shape[0] > 8:
            half = t.shape[0] // 2
            t = op(t[:half], t[half:])
        parts.append(t)
    while len(parts) > 1:
        parts = [op(parts[i], parts[i + 1]) if i + 1 < len(parts) else parts[i]
                 for i in range(0, len(parts), 2)]
    return parts[0]


def _moba_body(q_ref, k_ref, v_ref, km_ref, yb_ref, vt_scr, qmask_scr, *s_slots, seq):
    nblk = seq // MOBA_BLOCK
    hmask = _head_masks(MOBA_BLOCK)
    key = lax.broadcasted_iota(jnp.int32, (MOBA_BLOCK, MOBA_BLOCK), 0)
    qry = lax.broadcasted_iota(jnp.int32, (MOBA_BLOCK, MOBA_BLOCK), 1)
    causal = key <= qry
    km = km_ref[...]
    km_hi = km.astype(BF16)
    km_lo = (km - km_hi.astype(F32)).astype(BF16)
    blk_id = lax.broadcasted_iota(jnp.int32, (nblk, MOBA_BLOCK), 0)
    for h in range(HEADS_PER_GROUP):
        qmask_scr[h] = jnp.where(hmask[h], HEAD_DIM ** -0.5, 0.0).astype(BF16)
    heads = range(HEADS_PER_GROUP)
    for h in heads:
        vt_scr[h * VT_ROWS + HEAD_DIM:(h + 1) * VT_ROWS, :] = jnp.ones((VT_ROWS - HEAD_DIM, seq), BF16)
    for j in range(nblk):
        blk = slice(j * MOBA_BLOCK, (j + 1) * MOBA_BLOCK)
        vt = v_ref[blk, :].T
        for h in heads:
            vt_scr[h * VT_ROWS:h * VT_ROWS + HEAD_DIM, blk] = vt[h * HEAD_DIM:(h + 1) * HEAD_DIM]

    units = [(j, n, h) for j in range(nblk) for n in [j] + list(range(j)) for h in heads]
    qh, pen = {}, {}

    def scores(unit):
        j, n, h = unit
        if (j, h) not in qh:
            qh[j, h] = q_ref[j * MOBA_BLOCK:(j + 1) * MOBA_BLOCK, :] * qmask_scr[h]
        return _dot_nt(k_ref[n * MOBA_BLOCK:(n + 1) * MOBA_BLOCK, :], qh[j, h])

    def penalty(j, h):
        if (j, h) not in pen:
            gate = _dot_nt(km_hi, qh[j, h]) + _dot_nt(km_lo, qh[j, h])
            beaten = jnp.zeros((nblk, MOBA_BLOCK), F32)
            for m in range(j):
                gm = gate[m:m + 1, :]
                beats = (gm > gate) | ((gm == gate) & (blk_id > m))
                beaten = beaten + jnp.where(beats, 1.0, 0.0)
            pen[j, h] = jnp.where(beaten >= MOBA_TOPK, NEG, 0.0)
        return pen[j, h]

    nslot = len(s_slots)
    for i in range(min(MOBA_LOOKAHEAD, len(units))):
        s_slots[i % nslot][...] = scores(units[i])
    m_run, l_run, acc = {}, {}, {}
    for i, (j, n, h) in enumerate(units):
        if i + MOBA_LOOKAHEAD < len(units):
            s_slots[(i + MOBA_LOOKAHEAD) % nslot][...] = scores(units[i + MOBA_LOOKAHEAD])
        s = s_slots[i % nslot][...]
        if n == j:
            s = jnp.where(causal, s, NEG)
        elif j > MOBA_TOPK:
            s = s + penalty(j, h)[n:n + 1, :]
        m_blk = jnp.max(_fold_rows([s], jnp.maximum), axis=0, keepdims=True)
        m_new = m_blk if n == j else jnp.maximum(m_run[h], m_blk)
        p = jnp.exp(s - m_new).astype(BF16)
        vt = vt_scr[h * VT_ROWS:(h + 1) * VT_ROWS, n * MOBA_BLOCK:(n + 1) * MOBA_BLOCK]
        pv = _dot(vt, p)
        acc[h] = pv if n == j else jnp.exp(m_run[h] - m_new) * acc[h] + pv
        m_run[h] = m_new
        if n == max(j - 1, 0) and h == HEADS_PER_GROUP - 1:
            out_t = jnp.concatenate(
                [acc[hh][:HEAD_DIM] * (1.0 / acc[hh][HEAD_DIM:HEAD_DIM + 1]) for hh in heads], axis=0)
            yb_ref[j * MOBA_BLOCK:(j + 1) * MOBA_BLOCK, :] = out_t.T.astype(BF16)


def _moba_call(bqkv, kmean, batch, seq):
    def qkv_spec(t):
        return pl.BlockSpec((None, seq, GROUP_W), lambda b, hg: (t, b, hg))

    nblk = seq // MOBA_BLOCK
    return pl.pallas_call(
        functools.partial(_moba_body, seq=seq),
        grid=(batch, WIDTH_B // GROUP_W),
        in_specs=[qkv_spec(0), qkv_spec(1), qkv_spec(2),
                  pl.BlockSpec((None, nblk, GROUP_W), lambda b, hg: (b, 0, hg))],
        out_specs=pl.BlockSpec((seq, GROUP_W), lambda b, hg: (b, hg)),
        out_shape=jax.ShapeDtypeStruct((batch * seq, WIDTH_B), BF16),
        scratch_shapes=[pltpu.VMEM((HEADS_PER_GROUP * VT_ROWS, seq), BF16),
                        pltpu.VMEM((HEADS_PER_GROUP, MOBA_BLOCK, GROUP_W), BF16)]
                       + [pltpu.VMEM((MOBA_BLOCK, MOBA_BLOCK), F32)] * (MOBA_LOOKAHEAD + 1),
        compiler_params=_params(2),
        name="moba_attn",
    )(bqkv, bqkv, bqkv, kmean)


def _mix_body(x_ref, ya_ref, yb_ref, gpre_ref, wg_ref, bg_ref, wpa_ref, wpb_ref, wo_ref, gpost_ref,
              out_ref):
    x = x_ref[...]
    h = _rms(x, gpre_ref[...]).astype(BF16)
    bias = bg_ref[...]
    gate_a = jax.nn.sigmoid(_dot(h, wg_ref[:, QKV_COLS:QKV_COLS + D_MODEL]) + bias[:, :D_MODEL])
    merged = gate_a * _dot(ya_ref[...], wpa_ref[...])
    gate_b = jax.nn.sigmoid(_dot(h, wg_ref[:, QKV_COLS + D_MODEL:]) + bias[:, D_MODEL:])
    merged = merged + gate_b * _dot(yb_ref[...], wpb_ref[...])
    out_ref[...] = x + _rms(_dot(merged.astype(BF16), wo_ref[...]), gpost_ref[...])


def _mix_call(layer, x, ya, yb, gpre, wg, bg, wpa, wpb, wo, gpost):
    tokens = x.shape[0]

    def full(shape):
        return pl.BlockSpec((None,) + shape, lambda i: (layer,) + (0,) * len(shape))

    def rows(width):
        return pl.BlockSpec((TM, width), lambda i: (i, 0))

    return pl.pallas_call(
        _mix_body,
        grid=(tokens // TM,),
        in_specs=[rows(D_MODEL), rows(GROUP_W), rows(WIDTH_B), full((1, D_MODEL)),
                  full((D_MODEL, D_IN)), full((1, 2 * D_MODEL)),
                  full((GROUP_W, D_MODEL)), full((WIDTH_B, D_MODEL)), full((D_MODEL, D_MODEL)),
                  full((1, D_MODEL))],
        out_specs=rows(D_MODEL),
        out_shape=jax.ShapeDtypeStruct((tokens, D_MODEL), F32),
        compiler_params=_params(1),
        name="post_mixer",
    )(x, ya, yb, gpre, wg, bg, wpa, wpb, wo, gpost)


def _ffn_body(x_ref, halo_ref, gpre_ref, wug_ref, wuv_ref, cwg_ref, cwv_ref, cbg_ref, cbv_ref,
              wd_ref, gpost_ref, out_ref, h_scr, acc_scr, act_scr, slab_scr, *u_slots, tiles_per_seq):
    i = pl.program_id(0)
    c = pl.program_id(1)
    n_slab = D_MODEL // LANES

    @pl.when(c == 0)
    def _():
        g = gpre_ref[...]
        keep = jnp.where(i % tiles_per_seq == 0, 0.0, 1.0)
        h_scr[0:HALO, :] = (_rms(halo_ref[...], g) * keep).astype(BF16)
        h = _rms(x_ref[...], g)
        for run in range(SUBLANES):
            for sl in range(n_slab):
                slab_scr[sl, pl.ds(run, FFN_RUN, stride=SUBLANES), :] = h[run * FFN_RUN:(run + 1) * FFN_RUN,
                                                                          sl * LANES:(sl + 1) * LANES]
        h_scr[HALO:, :] = jnp.concatenate([slab_scr[sl] for sl in range(n_slab)], axis=-1).astype(BF16)
        acc_scr[...] = jnp.zeros_like(acc_scr)

    h = h_scr[...]
    n_sub = FF_CHUNK // FF_SUB

    def up(s):
        cols = slice(s * FF_SUB, (s + 1) * FF_SUB)
        u_slots[s % 2][0] = _dot(h, wug_ref[:, cols])
        u_slots[s % 2][1] = _dot(h, wuv_ref[:, cols])

    first_sublane = lax.broadcasted_iota(jnp.int32, (SUBLANES, FF_SUB), 0) == 0

    def conv(u, cw_ref, cb_ref, cols):
        cw = cw_ref[:, cols]

        def before_first(back):
            last = u[HALO + TM_FFN - back * SUBLANES:HALO + TM_FFN - (back - 1) * SUBLANES]
            return jnp.where(first_sublane, u[HALO - back:HALO - back + 1], pltpu.roll(last, 1, 0))

        ext = jnp.concatenate([before_first(2), before_first(1), u[HALO:]], axis=0)
        return (ext[0:TM_FFN] * cw[0:1] + ext[SUBLANES:SUBLANES + TM_FFN] * cw[1:2]
                + ext[2 * SUBLANES:] * cw[2:3] + cb_ref[:, cols])

    def down(half):
        rows = slice(half * (FF_CHUNK // 2), (half + 1) * (FF_CHUNK // 2))
        acc_scr[...] += _dot(act_scr[:, rows], wd_ref[rows, :])

    up(0)
    for s in range(n_sub):
        if s + 1 < n_sub:
            up(s + 1)
        if s == n_sub // 2 + 1:
            down(0)
        cols = slice(s * FF_SUB, (s + 1) * FF_SUB)
        gate = conv(u_slots[s % 2][0], cwg_ref, cbg_ref, cols)
        val = conv(u_slots[s % 2][1], cwv_ref, cbv_ref, cols)
        act_scr[:, cols] = (jax.nn.gelu(gate, approximate=True) * val).astype(BF16)
    down(1)

    @pl.when(c == pl.num_programs(1) - 1)
    def _():
        y = _rms(acc_scr[...], gpost_ref[...])
        for sl in range(n_slab):
            slab_scr[sl] = y[:, sl * LANES:(sl + 1) * LANES]
        for run in range(SUBLANES):
            rows = slice(run * FFN_RUN, (run + 1) * FFN_RUN)
            y_run = jnp.concatenate([slab_scr[sl, pl.ds(run, FFN_RUN, stride=SUBLANES), :]
                                     for sl in range(n_slab)], axis=-1)
            out_ref[rows, :] = x_ref[rows, :] + y_run


def _ffn_call(layer, x, gpre, wu, cw, cb, wd, gpost, seq):
    tokens = x.shape[0]
    n_chunks = D_FF // FF_CHUNK
    halo_per_tile = TM_FFN // HALO
    vec = pl.BlockSpec((None, 1, D_MODEL), lambda i, c: (layer, 0, 0))
    return pl.pallas_call(
        functools.partial(_ffn_body, tiles_per_seq=seq // TM_FFN),
        grid=(tokens // TM_FFN, n_chunks),
        in_specs=[pl.BlockSpec((TM_FFN, D_MODEL), lambda i, c: (i, 0)),
                  pl.BlockSpec((HALO, D_MODEL), lambda i, c: (jnp.maximum(i * halo_per_tile - 1, 0), 0)),
                  vec,
                  pl.BlockSpec((None, D_MODEL, FF_CHUNK), lambda i, c: (layer, 0, c)),
                  pl.BlockSpec((None, D_MODEL, FF_CHUNK), lambda i, c: (layer, 0, n_chunks + c)),
                  pl.BlockSpec((None, 3, FF_CHUNK), lambda i, c: (layer, 0, c)),
                  pl.BlockSpec((None, 3, FF_CHUNK), lambda i, c: (layer, 0, n_chunks + c)),
                  pl.BlockSpec((None, 1, FF_CHUNK), lambda i, c: (layer, 0, c)),
                  pl.BlockSpec((None, 1, FF_CHUNK), lambda i, c: (layer, 0, n_chunks + c)),
                  pl.BlockSpec((None, FF_CHUNK, D_MODEL), lambda i, c: (layer, c, 0)),
                  vec],
        out_specs=pl.BlockSpec((TM_FFN, D_MODEL), lambda i, c: (i, 0)),
        out_shape=jax.ShapeDtypeStruct((tokens, D_MODEL), F32),
        scratch_shapes=[pltpu.VMEM((HALO + TM_FFN, D_MODEL), BF16), pltpu.VMEM((TM_FFN, D_MODEL), F32),
                        pltpu.VMEM((TM_FFN, FF_CHUNK), BF16),
                        pltpu.VMEM((D_MODEL // LANES, TM_FFN, LANES), F32)]
                       + [pltpu.VMEM((2, HALO + TM_FFN, FF_SUB), F32)] * 2,
        compiler_params=_params(2),
        name="conv_ffn",
    )(x, x, gpre, wu, wu, cw, cw, cb, cb, wd, gpost)


def _rope_tables(positions):
    inv_freq = ROPE_THETA ** (-jnp.arange(0, ROT_DIM, 2, dtype=F32) / ROT_DIM)
    lane = jnp.arange(LANES, dtype=jnp.int32) % HEAD_DIM
    ang = positions.astype(F32).reshape(-1, 1) * inv_freq[lane % ROT_HALF][None, :]
    cos = jnp.where(lane < ROT_DIM, jnp.cos(ang), 1.0)
    sin = jnp.sin(ang)
    return (cos, jnp.where(lane < ROT_HALF, -sin, 0.0),
            jnp.where((lane >= ROT_HALF) & (lane < ROT_DIM), sin, 0.0))


def kernel(x, positions, norm_pre_mix, w_in, b_gate, w_proj_a, w_proj_b, w_out, norm_post_mix,
           norm_pre_ffn, w_up, conv_w, conv_b, w_down, norm_post_ffn):
    batch, seq, _ = x.shape
    depth = w_in.shape[0]
    assert seq % TM == 0 and seq % TM_FFN == 0 and TM % MOBA_BLOCK == 0 and seq // DIL_PAIRS[-1][1] == BAND
    cos, sin_lo, sin_hi = _rope_tables(positions)
    w_i = w_in.astype(BF16)
    w_pa, w_pb, w_o = w_proj_a.astype(BF16), w_proj_b.astype(BF16), w_out.astype(BF16)
    w_u, w_d = w_up.astype(BF16), w_down.astype(BF16)
    rows = lambda p: p.reshape(depth, 1, -1)
    g_pre_mix, g_post_mix = rows(norm_pre_mix), rows(norm_post_mix)
    g_pre_ffn, g_post_ffn = rows(norm_pre_ffn), rows(norm_post_ffn)
    bias_gate, bias_conv = rows(b_gate), rows(conv_b)

    xt = x.reshape(batch * seq, D_MODEL)
    for l in range(depth):
        a0, a1, a2, bqkv, kmean = _qkv_call(l, xt, g_pre_mix, w_i, cos, sin_lo, sin_hi, batch, seq)
        ya = _dilated_call(a0, a1, a2, batch, seq)
        yb = _moba_call(bqkv, kmean.reshape(batch, seq // MOBA_BLOCK, WIDTH_B), batch, seq)
        xt = _mix_call(l, xt, ya, yb, g_pre_mix, w_i, bias_gate, w_pa, w_pb, w_o, g_post_mix)
        xt = _ffn_call(l, xt, g_pre_ffn, w_u, conv_w, bias_conv, w_d, g_post_ffn, seq)
    return xt.reshape(batch, seq, D_MODEL)
```

```python
import functools

import jax
import jax.numpy as jnp
from jax import lax
from jax.experimental import pallas as pl
from jax.experimental.pallas import tpu as pltpu

F32 = jnp.float32
BF16 = jnp.bfloat16

D_MODEL = 1024
HEAD_DIM = 64
DIL_PAIRS = ((128, 1), (512, 4), (2048, 16))
BAND = 128
HEADS_PER_GROUP = 4
GROUP_W = HEADS_PER_GROUP * HEAD_DIM
WIDTH_A = 3 * GROUP_W
WIDTH_B = 512
MOBA_BLOCK = 256
MOBA_TOPK = 3
ROPE_THETA = 500000.0
ROT_DIM = HEAD_DIM // 4
ROT_HALF = ROT_DIM // 2
D_FF = 4 * D_MODEL
RMS_EPS = 1e-6
QKV_COLS = 3 * WIDTH_A + 3 * WIDTH_B
D_IN = QKV_COLS + 2 * D_MODEL
LANES = 128
NEG = -1e30

TM = 512
TM_FFN = 1024
SUBLANES = 8
FFN_RUN = TM_FFN // SUBLANES
FF_CHUNK = 1024
FF_SUB = 256
MOBA_Q_SCALE = HEAD_DIM ** -0.5 * 1.4426950408889634
MOBA_LOOKAHEAD = 7
VT_ROWS = HEAD_DIM + 16
DIL_UNROLL = 4
HALO = 16
VMEM_LIMIT = 56 * 1024 * 1024


def _rms(x, g):
    return x * lax.rsqrt(jnp.mean(x * x, axis=-1, keepdims=True) + RMS_EPS) * g


def _dot(a, b):
    return jnp.dot(a, b, preferred_element_type=F32)


def _dot_nt(a, b):
    return lax.dot_general(a, b, (((1,), (1,)), ((), ())), preferred_element_type=F32)


def _params(n_grid):
    return pltpu.CompilerParams(dimension_semantics=("arbitrary",) * n_grid,
                                vmem_limit_bytes=VMEM_LIMIT)


def _qkv_body(x_ref, g_ref, w_ref, c_ref, s1_ref, s2_ref,
              a0_ref, a1_ref, a2_ref, b_ref, km_ref, res_ref):
    h = _rms(x_ref[...], g_ref[...]).astype(BF16)
    cos = jnp.concatenate([c_ref[...]] * 2, axis=-1)
    sin_lo = jnp.concatenate([s1_ref[...]] * 2, axis=-1)
    sin_hi = jnp.concatenate([s2_ref[...]] * 2, axis=-1)

    def proj(c0, rope):
        t = _dot(h, w_ref[:, c0:c0 + GROUP_W])
        if rope:
            t = (t * cos + pltpu.roll(t, GROUP_W - ROT_HALF, 1) * sin_lo
                 + pltpu.roll(t, ROT_HALF, 1) * sin_hi)
        return t

    a_refs = (a0_ref, a1_ref, a2_ref)
    for g, (_, dil) in enumerate(DIL_PAIRS):
        for t in range(3):
            res = proj(t * WIDTH_A + g * GROUP_W, rope=t < 2)
            if dil == 1:
                a_refs[g][t, 0, 0] = res.astype(BF16)
            else:
                res_ref[0] = res[:, :LANES]
                res_ref[1] = res[:, LANES:]
                n = TM // dil
                for r in range(dil):
                    piece = jnp.concatenate(
                        [res_ref[0, pl.ds(r, n, stride=dil), :],
                         res_ref[1, pl.ds(r, n, stride=dil), :]], axis=-1)
                    a_refs[g][t, 0, r] = piece.astype(BF16)
    for t in range(3):
        for cc in range(WIDTH_B // GROUP_W):
            lo = cc * GROUP_W
            res = proj(3 * WIDTH_A + t * WIDTH_B + lo, rope=t < 2)
            if t == 0:
                res = res * MOBA_Q_SCALE
            b_ref[t, :, lo:lo + GROUP_W] = res.astype(BF16)
            if t == 1:
                for blk in range(TM // MOBA_BLOCK):
                    km_ref[0, blk:blk + 1, lo:lo + GROUP_W] = jnp.mean(
                        res[blk * MOBA_BLOCK:(blk + 1) * MOBA_BLOCK], axis=0, keepdims=True)


def _qkv_call(layer, x, g, w, cos, sin_lo, sin_hi, batch, seq):
    tokens = x.shape[0]
    tiles_per_seq = seq // TM
    tab_spec = pl.BlockSpec((TM, LANES), lambda i: (i, 0))

    def a_spec(dil):
        return pl.BlockSpec((3, 1, dil, TM // dil, GROUP_W),
                            lambda i: (0, i // tiles_per_seq, 0, i % tiles_per_seq, 0))

    out_shape = [jax.ShapeDtypeStruct((3, batch, dil, seq // dil, GROUP_W), BF16) for _, dil in DIL_PAIRS]
    out_shape += [jax.ShapeDtypeStruct((3, tokens, WIDTH_B), BF16),
                  jax.ShapeDtypeStruct((tokens // TM, TM // MOBA_BLOCK, WIDTH_B), F32)]
    out_specs = [a_spec(dil) for _, dil in DIL_PAIRS]
    out_specs += [pl.BlockSpec((3, TM, WIDTH_B), lambda i: (0, i, 0)),
                  pl.BlockSpec((1, TM // MOBA_BLOCK, WIDTH_B), lambda i: (i, 0, 0))]
    return pl.pallas_call(
        _qkv_body,
        grid=(tokens // TM,),
        in_specs=[pl.BlockSpec((TM, D_MODEL), lambda i: (i, 0)),
                  pl.BlockSpec((None, 1, D_MODEL), lambda i: (layer, 0, 0)),
                  pl.BlockSpec((None, D_MODEL, QKV_COLS), lambda i: (layer, 0, 0)),
                  tab_spec, tab_spec, tab_spec],
        out_specs=out_specs,
        out_shape=out_shape,
        scratch_shapes=[pltpu.VMEM((2, TM, LANES), F32)],
        compiler_params=_params(1),
        name="qkv_proj",
    )(x, g, w, cos, sin_lo, sin_hi)


def _head_masks(rows):
    lane = lax.broadcasted_iota(jnp.int32, (rows, GROUP_W), 1)
    return [(lane >= h * HEAD_DIM) & (lane < (h + 1) * HEAD_DIM) for h in range(HEADS_PER_GROUP)]


def _dilated_body(a0_ref, a1_ref, a2_ref, ya_ref, o_scr, lse_scr, qmask_scr, bias_scr, *s_slots, seq):
    hmask = _head_masks(BAND)
    row = lax.broadcasted_iota(jnp.int32, (BAND, 2 * BAND), 0)
    col = lax.broadcasted_iota(jnp.int32, (BAND, 2 * BAND), 1)
    band = ((col < BAND) & (col >= row)) | ((col >= BAND) & (col - BAND <= row))
    bias_scr[1] = jnp.where(band, 0.0, NEG)
    bias_scr[0] = jnp.where(band & (col >= BAND), 0.0, NEG)
    for h in range(HEADS_PER_GROUP):
        qmask_scr[h] = jnp.where(hmask[h], HEAD_DIM ** -0.5, 0.0).astype(BF16)

    for g, (a_ref, (_, dil)) in enumerate(zip((a0_ref, a1_ref, a2_ref), DIL_PAIRS)):
        nb = seq // dil // BAND
        has_prev = nb > 1

        nk = 2 * BAND if has_prev else BAND

        def key_rows(idx, nb=nb):
            r, n = idx // nb, idx % nb
            return r, n, pl.multiple_of(n * BAND, BAND), pl.multiple_of(jnp.maximum(n - 1, 0) * BAND, BAND)

        def gather(a_ref, t, r, row0, rowp, has_prev=has_prev):
            cur = a_ref[t, 0, r, pl.ds(row0, BAND), :]
            if has_prev:
                return jnp.concatenate([a_ref[t, 0, r, pl.ds(rowp, BAND), :], cur], axis=0)
            return cur

        def blocks(it, carry, a_ref=a_ref, dil=dil, has_prev=has_prev, g=g, nk=nk):
            for u in range(DIL_UNROLL):
                r, n, row0, rowp = key_rows(it * DIL_UNROLL + u)
                q = a_ref[0, 0, r, pl.ds(row0, BAND), :]
                q4 = jnp.concatenate([q * qmask_scr[h] for h in range(HEADS_PER_GROUP)], axis=0)
                bias = bias_scr[jnp.minimum(n, 1)] if has_prev else bias_scr[1, :, BAND:]
                s = _dot_nt(q4, gather(a_ref, 1, r, row0, rowp)).reshape(HEADS_PER_GROUP, BAND, nk)
                s_slots[u][:, 0:nk] = (s + bias[None]).reshape(HEADS_PER_GROUP * BAND, nk)
            for u in range(DIL_UNROLL):
                r, n, row0, rowp = key_rows(it * DIL_UNROLL + u)
                s = s_slots[u][:, 0:nk]
                m = jnp.max(s, axis=-1, keepdims=True)
                p = jnp.exp(s - m)
                l = jnp.sum(p, axis=-1, keepdims=True)
                o4 = _dot(p.astype(BF16), gather(a_ref, 2, r, row0, rowp)) * (1.0 / l)
                lse4 = jnp.broadcast_to(m + jnp.log(l), (HEADS_PER_GROUP * BAND, GROUP_W))
                o = o4[0:BAND]
                lse = lse4[0:BAND]
                for h in range(1, HEADS_PER_GROUP):
                    o = jnp.where(hmask[h], o4[h * BAND:(h + 1) * BAND], o)
                    lse = jnp.where(hmask[h], lse4[h * BAND:(h + 1) * BAND], lse)
                start = n * (BAND * dil) + r
                for slab in range(GROUP_W // LANES):
                    if dil == 1:
                        dst = pl.ds(pl.multiple_of(start, BAND), BAND)
                    else:
                        dst = pl.ds(start, BAND, stride=dil)
                    o_scr[g, slab, dst, :] = o[:, slab * LANES:(slab + 1) * LANES]
                    lse_scr[g, slab, dst, :] = lse[:, slab * LANES:(slab + 1) * LANES]
            return carry

        lax.fori_loop(0, dil * nb // DIL_UNROLL, blocks, 0)

    chunk = 256
    for c in range(seq // chunk):
        rows = pl.ds(c * chunk, chunk)
        for slab in range(GROUP_W // LANES):
            lses = [lse_scr[g, slab, rows, :] for g in range(len(DIL_PAIRS))]
            top = jnp.maximum(jnp.maximum(lses[0], lses[1]), lses[2])
            es = [jnp.exp(l - top) for l in lses]
            num = sum(e * o_scr[g, slab, rows, :] for g, e in enumerate(es))
            ya_ref[rows, slab * LANES:(slab + 1) * LANES] = (num / (es[0] + es[1] + es[2])).astype(BF16)


def _dilated_call(a0, a1, a2, batch, seq):
    def a_spec(dil):
        return pl.BlockSpec((3, 1, dil, seq // dil, GROUP_W), lambda b: (0, b, 0, 0, 0))

    n_groups = len(DIL_PAIRS)
    return pl.pallas_call(
        functools.partial(_dilated_body, seq=seq),
        grid=(batch,),
        in_specs=[a_spec(dil) for _, dil in DIL_PAIRS],
        out_specs=pl.BlockSpec((seq, GROUP_W), lambda b: (b, 0)),
        out_shape=jax.ShapeDtypeStruct((batch * seq, GROUP_W), BF16),
        scratch_shapes=[pltpu.VMEM((n_groups, GROUP_W // LANES, seq, LANES), F32),
                        pltpu.VMEM((n_groups, GROUP_W // LANES, seq, LANES), F32),
                        pltpu.VMEM((HEADS_PER_GROUP, BAND, GROUP_W), BF16),
                        pltpu.VMEM((2, BAND, 2 * BAND), F32)]
                       + [pltpu.VMEM((HEADS_PER_GROUP * BAND, 2 * BAND), F32)] * DIL_UNROLL,
        compiler_params=_params(1),
        name="dilated_attn",
    )(a0, a1, a2)


def _fold_rows(tiles, op):
    parts = []
    for t in tiles:
        while t.shape[0] > 8:
            half = t.shape[0] // 2
            t = op(t[:half], t[half:])
        parts.append(t)
    while len(parts) > 1:
        parts = [op(parts[i], parts[i + 1]) if i + 1 < len(parts) else parts[i]
                 for i in range(0, len(parts), 2)]
    return parts[0]


def _moba_body(q_ref, k_ref, v_ref, km_ref, yb_ref, vt_scr, qmask_scr, *s_slots, seq):
    nblk = seq // MOBA_BLOCK
    hmask = _head_masks(MOBA_BLOCK)
    key = lax.broadcasted_iota(jnp.int32, (MOBA_BLOCK, MOBA_BLOCK), 0)
    qry = lax.broadcasted_iota(jnp.int32, (MOBA_BLOCK, MOBA_BLOCK), 1)
    causal = key <= qry
    km = km_ref[...]
    km_hi = km.astype(BF16)
    km_lo = (km - km_hi.astype(F32)).astype(BF16)
    blk_id = lax.broadcasted_iota(jnp.int32, (nblk, MOBA_BLOCK), 0)
    for h in range(HEADS_PER_GROUP):
        qmask_scr[h] = jnp.where(hmask[h], 1.0, 0.0).astype(BF16)
    heads = range(HEADS_PER_GROUP)
    for h in heads:
        vt_scr[h * VT_ROWS + HEAD_DIM:(h + 1) * VT_ROWS, :] = jnp.ones((VT_ROWS - HEAD_DIM, seq), BF16)
    for j in range(nblk):
        blk = slice(j * MOBA_BLOCK, (j + 1) * MOBA_BLOCK)
        vt = v_ref[blk, :].T
        for h in heads:
            vt_scr[h * VT_ROWS:h * VT_ROWS + HEAD_DIM, blk] = vt[h * HEAD_DIM:(h + 1) * HEAD_DIM]

    units = [(j, n, h) for j in range(nblk) for n in [j] + list(range(j)) for h in heads]
    qh, pen = {}, {}

    def scores(unit):
        j, n, h = unit
        if (j, h) not in qh:
            qh[j, h] = q_ref[j * MOBA_BLOCK:(j + 1) * MOBA_BLOCK, :] * qmask_scr[h]
        return _dot_nt(k_ref[n * MOBA_BLOCK:(n + 1) * MOBA_BLOCK, :], qh[j, h])

    def penalty(j, h):
        if (j, h) not in pen:
            gate = _dot_nt(km_hi, qh[j, h]) + _dot_nt(km_lo, qh[j, h])
            beaten = jnp.zeros((nblk, MOBA_BLOCK), F32)
            for m in range(j):
                gm = gate[m:m + 1, :]
                beats = (gm > gate) | ((gm == gate) & (blk_id > m))
                beaten = beaten + jnp.where(beats, 1.0, 0.0)
            pen[j, h] = jnp.where(beaten >= MOBA_TOPK, NEG, 0.0)
        return pen[j, h]

    nslot = len(s_slots)
    for i in range(min(MOBA_LOOKAHEAD, len(units))):
        s_slots[i % nslot][...] = scores(units[i])
    m_run, l_run, acc = {}, {}, {}
    for i, (j, n, h) in enumerate(units):
        if i + MOBA_LOOKAHEAD < len(units):
            s_slots[(i + MOBA_LOOKAHEAD) % nslot][...] = scores(units[i + MOBA_LOOKAHEAD])
        s = s_slots[i % nslot][...]
        if n == j:
            s = jnp.where(causal, s, NEG)
        elif j > MOBA_TOPK:
            s = s + penalty(j, h)[n:n + 1, :]
        m_blk = jnp.max(_fold_rows([s], jnp.maximum), axis=0, keepdims=True)
        m_new = m_blk if n == j else jnp.maximum(m_run[h], m_blk)
        p = jnp.exp2(s - m_new).astype(BF16)
        vt = vt_scr[h * VT_ROWS:(h + 1) * VT_ROWS, n * MOBA_BLOCK:(n + 1) * MOBA_BLOCK]
        pv = _dot(vt, p)
        acc[h] = pv if n == j else jnp.exp2(m_run[h] - m_new) * acc[h] + pv
        m_run[h] = m_new
        if n == max(j - 1, 0) and h == HEADS_PER_GROUP - 1:
            out_t = jnp.concatenate(
                [acc[hh][:HEAD_DIM] * (1.0 / acc[hh][HEAD_DIM:HEAD_DIM + 1]) for hh in heads], axis=0)
            yb_ref[j * MOBA_BLOCK:(j + 1) * MOBA_BLOCK, :] = out_t.T.astype(BF16)


def _moba_call(bqkv, kmean, batch, seq):
    def qkv_spec(t):
        return pl.BlockSpec((None, seq, GROUP_W), lambda b, hg: (t, b, hg))

    nblk = seq // MOBA_BLOCK
    return pl.pallas_call(
        functools.partial(_moba_body, seq=seq),
        grid=(batch, WIDTH_B // GROUP_W),
        in_specs=[qkv_spec(0), qkv_spec(1), qkv_spec(2),
                  pl.BlockSpec((None, nblk, GROUP_W), lambda b, hg: (b, 0, hg))],
        out_specs=pl.BlockSpec((seq, GROUP_W), lambda b, hg: (b, hg)),
        out_shape=jax.ShapeDtypeStruct((batch * seq, WIDTH_B), BF16),
        scratch_shapes=[pltpu.VMEM((HEADS_PER_GROUP * VT_ROWS, seq), BF16),
                        pltpu.VMEM((HEADS_PER_GROUP, MOBA_BLOCK, GROUP_W), BF16)]
                       + [pltpu.VMEM((MOBA_BLOCK, MOBA_BLOCK), F32)] * (MOBA_LOOKAHEAD + 1),
        compiler_params=_params(2),
        name="moba_attn",
    )(bqkv, bqkv, bqkv, kmean)


def _mix_body(x_ref, ya_ref, yb_ref, gpre_ref, wg_ref, bg_ref, wpa_ref, wpb_ref, wo_ref, gpost_ref,
              out_ref):
    x = x_ref[...]
    h = _rms(x, gpre_ref[...]).astype(BF16)
    bias = bg_ref[...]
    gate_a = jax.nn.sigmoid(_dot(h, wg_ref[:, QKV_COLS:QKV_COLS + D_MODEL]) + bias[:, :D_MODEL])
    merged = gate_a * _dot(ya_ref[...], wpa_ref[...])
    gate_b = jax.nn.sigmoid(_dot(h, wg_ref[:, QKV_COLS + D_MODEL:]) + bias[:, D_MODEL:])
    merged = merged + gate_b * _dot(yb_ref[...], wpb_ref[...])
    out_ref[...] = x + _rms(_dot(merged.astype(BF16), wo_ref[...]), gpost_ref[...])


def _mix_call(layer, x, ya, yb, gpre, wg, bg, wpa, wpb, wo, gpost):
    tokens = x.shape[0]

    def full(shape):
        return pl.BlockSpec((None,) + shape, lambda i: (layer,) + (0,) * len(shape))

    def rows(width):
        return pl.BlockSpec((TM, width), lambda i: (i, 0))

    return pl.pallas_call(
        _mix_body,
        grid=(tokens // TM,),
        in_specs=[rows(D_MODEL), rows(GROUP_W), rows(WIDTH_B), full((1, D_MODEL)),
                  full((D_MODEL, D_IN)), full((1, 2 * D_MODEL)),
                  full((GROUP_W, D_MODEL)), full((WIDTH_B, D_MODEL)), full((D_MODEL, D_MODEL)),
                  full((1, D_MODEL))],
        out_specs=rows(D_MODEL),
        out_shape=jax.ShapeDtypeStruct((tokens, D_MODEL), F32),
        compiler_params=_params(1),
        name="post_mixer",
    )(x, ya, yb, gpre, wg, bg, wpa, wpb, wo, gpost)


def _ffn_body(x_ref, halo_ref, gpre_ref, wug_ref, wuv_ref, cwg_ref, cwv_ref, cbg_ref, cbv_ref,
              wd_ref, gpost_ref, out_ref, h_scr, acc_scr, act_scr, slab_scr, *u_slots, tiles_per_seq):
    i = pl.program_id(0)
    c = pl.program_id(1)
    n_slab = D_MODEL // LANES

    @pl.when(c == 0)
    def _():
        g = gpre_ref[...]
        keep = jnp.where(i % tiles_per_seq == 0, 0.0, 1.0)
        h_scr[0:HALO, :] = (_rms(halo_ref[...], g) * keep).astype(BF16)
        h = _rms(x_ref[...], g)
        for run in range(SUBLANES):
            for sl in range(n_slab):
                slab_scr[sl, pl.ds(run, FFN_RUN, stride=SUBLANES), :] = h[run * FFN_RUN:(run + 1) * FFN_RUN,
                                                                          sl * LANES:(sl + 1) * LANES]
        h_scr[HALO:, :] = jnp.concatenate([slab_scr[sl] for sl in range(n_slab)], axis=-1).astype(BF16)
        acc_scr[...] = jnp.zeros_like(acc_scr)

    h = h_scr[...]
    n_sub = FF_CHUNK // FF_SUB

    def up(s):
        cols = slice(s * FF_SUB, (s + 1) * FF_SUB)
        u_slots[s % 2][0] = _dot(h, wug_ref[:, cols])
        u_slots[s % 2][1] = _dot(h, wuv_ref[:, cols])

    first_sublane = lax.broadcasted_iota(jnp.int32, (SUBLANES, FF_SUB), 0) == 0

    def conv(u, cw_ref, cb_ref, cols):
        cw = cw_ref[:, cols]

        def before_first(back):
            last = u[HALO + TM_FFN - back * SUBLANES:HALO + TM_FFN - (back - 1) * SUBLANES]
            return jnp.where(first_sublane, u[HALO - back:HALO - back + 1], pltpu.roll(last, 1, 0))

        ext = jnp.concatenate([before_first(2), before_first(1), u[HALO:]], axis=0)
        return (ext[0:TM_FFN] * cw[0:1] + ext[SUBLANES:SUBLANES + TM_FFN] * cw[1:2]
                + ext[2 * SUBLANES:] * cw[2:3] + cb_ref[:, cols])

    def down(half):
        rows = slice(half * (FF_CHUNK // 2), (half + 1) * (FF_CHUNK // 2))
        acc_scr[...] += _dot(act_scr[:, rows], wd_ref[rows, :])

    up(0)
    for s in range(n_sub):
        if s + 1 < n_sub:
            up(s + 1)
        if s == n_sub // 2 + 1:
            down(0)
        cols = slice(s * FF_SUB, (s + 1) * FF_SUB)
        gate = conv(u_slots[s % 2][0], cwg_ref, cbg_ref, cols)
        val = conv(u_slots[s % 2][1], cwv_ref, cbv_ref, cols)
        act_scr[:, cols] = (jax.nn.gelu(gate, approximate=True) * val).astype(BF16)
    down(1)

    @pl.when(c == pl.num_programs(1) - 1)
    def _():
        y = _rms(acc_scr[...], gpost_ref[...])
        for sl in range(n_slab):
            slab_scr[sl] = y[:, sl * LANES:(sl + 1) * LANES]
        for run in range(SUBLANES):
            rows = slice(run * FFN_RUN, (run + 1) * FFN_RUN)
            y_run = jnp.concatenate([slab_scr[sl, pl.ds(run, FFN_RUN, stride=SUBLANES), :]
                                     for sl in range(n_slab)], axis=-1)
            out_ref[rows, :] = x_ref[rows, :] + y_run


def _ffn_call(layer, x, gpre, wu, cw, cb, wd, gpost, seq):
    tokens = x.shape[0]
    n_chunks = D_FF // FF_CHUNK
    halo_per_tile = TM_FFN // HALO
    vec = pl.BlockSpec((None, 1, D_MODEL), lambda i, c: (layer, 0, 0))
    return pl.pallas_call(
        functools.partial(_ffn_body, tiles_per_seq=seq // TM_FFN),
        grid=(tokens // TM_FFN, n_chunks),
        in_specs=[pl.BlockSpec((TM_FFN, D_MODEL), lambda i, c: (i, 0)),
                  pl.BlockSpec((HALO, D_MODEL), lambda i, c: (jnp.maximum(i * halo_per_tile - 1, 0), 0)),
                  vec,
                  pl.BlockSpec((None, D_MODEL, FF_CHUNK), lambda i, c: (layer, 0, c)),
                  pl.BlockSpec((None, D_MODEL, FF_CHUNK), lambda i, c: (layer, 0, n_chunks + c)),
                  pl.BlockSpec((None, 3, FF_CHUNK), lambda i, c: (layer, 0, c)),
                  pl.BlockSpec((None, 3, FF_CHUNK), lambda i, c: (layer, 0, n_chunks + c)),
                  pl.BlockSpec((None, 1, FF_CHUNK), lambda i, c: (layer, 0, c)),
                  pl.BlockSpec((None, 1, FF_CHUNK), lambda i, c: (layer, 0, n_chunks + c)),
                  pl.BlockSpec((None, FF_CHUNK, D_MODEL), lambda i, c: (layer, c, 0)),
                  vec],
        out_specs=pl.BlockSpec((TM_FFN, D_MODEL), lambda i, c: (i, 0)),
        out_shape=jax.ShapeDtypeStruct((tokens, D_MODEL), F32),
        scratch_shapes=[pltpu.VMEM((HALO + TM_FFN, D_MODEL), BF16), pltpu.VMEM((TM_FFN, D_MODEL), F32),
                        pltpu.VMEM((TM_FFN, FF_CHUNK), BF16),
                        pltpu.VMEM((D_MODEL // LANES, TM_FFN, LANES), F32)]
                       + [pltpu.VMEM((2, HALO + TM_FFN, FF_SUB), F32)] * 2,
        compiler_params=_params(2),
        name="conv_ffn",
    )(x, x, gpre, wu, wu, cw, cw, cb, cb, wd, gpost)


def _rope_tables(positions):
    inv_freq = ROPE_THETA ** (-jnp.arange(0, ROT_DIM, 2, dtype=F32) / ROT_DIM)
    lane = jnp.arange(LANES, dtype=jnp.int32) % HEAD_DIM
    ang = positions.astype(F32).reshape(-1, 1) * inv_freq[lane % ROT_HALF][None, :]
    cos = jnp.where(lane < ROT_DIM, jnp.cos(ang), 1.0)
    sin = jnp.sin(ang)
    return (cos, jnp.where(lane < ROT_HALF, -sin, 0.0),
            jnp.where((lane >= ROT_HALF) & (lane < ROT_DIM), sin, 0.0))


def kernel(x, positions, norm_pre_mix, w_in, b_gate, w_proj_a, w_proj_b, w_out, norm_post_mix,
           norm_pre_ffn, w_up, conv_w, conv_b, w_down, norm_post_ffn):
    batch, seq, _ = x.shape
    depth = w_in.shape[0]
    assert seq % TM == 0 and seq % TM_FFN == 0 and TM % MOBA_BLOCK == 0 and seq // DIL_PAIRS[-1][1] == BAND
    cos, sin_lo, sin_hi = _rope_tables(positions)
    w_i = w_in.astype(BF16)
    w_pa, w_pb, w_o = w_proj_a.astype(BF16), w_proj_b.astype(BF16), w_out.astype(BF16)
    w_u, w_d = w_up.astype(BF16), w_down.astype(BF16)
    rows = lambda p: p.reshape(depth, 1, -1)
    g_pre_mix, g_post_mix = rows(norm_pre_mix), rows(norm_post_mix)
    g_pre_ffn, g_post_ffn = rows(norm_pre_ffn), rows(norm_post_ffn)
    bias_gate, bias_conv = rows(b_gate), rows(conv_b)

    xt = x.reshape(batch * seq, D_MODEL)
    for l in range(depth):
        a0, a1, a2, bqkv, kmean = _qkv_call(l, xt, g_pre_mix, w_i, cos, sin_lo, sin_hi, batch, seq)
        ya = _dilated_call(a0, a1, a2, batch, seq)
        yb = _moba_call(bqkv, kmean.reshape(batch, seq // MOBA_BLOCK, WIDTH_B), batch, seq)
        xt = _mix_call(l, xt, ya, yb, g_pre_mix, w_i, bias_gate, w_pa, w_pb, w_o, g_post_mix)
        xt = _ffn_call(l, xt, g_pre_ffn, w_u, conv_w, bias_conv, w_d, g_post_ffn, seq)
    return xt.reshape(batch, seq, D_MODEL)
```

```python
import functools

import jax
import jax.numpy as jnp
from jax import lax
from jax.experimental import pallas as pl
from jax.experimental.pallas import tpu as pltpu

F32 = jnp.float32
BF16 = jnp.bfloat16

D_MODEL = 1024
HEAD_DIM = 64
DIL_PAIRS = ((128, 1), (512, 4), (2048, 16))
BAND = 128
HEADS_PER_GROUP = 4
GROUP_W = HEADS_PER_GROUP * HEAD_DIM
WIDTH_A = 3 * GROUP_W
WIDTH_B = 512
MOBA_BLOCK = 256
MOBA_TOPK = 3
ROPE_THETA = 500000.0
ROT_DIM = HEAD_DIM // 4
ROT_HALF = ROT_DIM // 2
D_FF = 4 * D_MODEL
RMS_EPS = 1e-6
QKV_COLS = 3 * WIDTH_A + 3 * WIDTH_B
D_IN = QKV_COLS + 2 * D_MODEL
LANES = 128
NEG = -1e30

TM = 512
TM_FFN = 1024
SUBLANES = 8
FFN_RUN = TM_FFN // SUBLANES
FF_CHUNK = 1024
FF_SUB = 256
MOBA_Q_SCALE = HEAD_DIM ** -0.5 * 1.4426950408889634
MOBA_LOOKAHEAD = 7
VT_ROWS = HEAD_DIM + 16
DIL_LOOKAHEAD = 2
HALO = 16
VMEM_LIMIT = 56 * 1024 * 1024


def _rms(x, g):
    return x * lax.rsqrt(jnp.mean(x * x, axis=-1, keepdims=True) + RMS_EPS) * g


def _dot(a, b):
    return jnp.dot(a, b, preferred_element_type=F32)


def _dot_nt(a, b):
    return lax.dot_general(a, b, (((1,), (1,)), ((), ())), preferred_element_type=F32)


def _params(n_grid):
    return pltpu.CompilerParams(dimension_semantics=("arbitrary",) * n_grid,
                                vmem_limit_bytes=VMEM_LIMIT)


def _qkv_body(x_ref, g_ref, w_ref, c_ref, s1_ref, s2_ref,
              a0_ref, a1_ref, a2_ref, b_ref, km_ref, res_ref):
    h = _rms(x_ref[...], g_ref[...]).astype(BF16)
    cos = jnp.concatenate([c_ref[...]] * 2, axis=-1)
    sin_lo = jnp.concatenate([s1_ref[...]] * 2, axis=-1)
    sin_hi = jnp.concatenate([s2_ref[...]] * 2, axis=-1)

    def proj(c0, rope):
        t = _dot(h, w_ref[:, c0:c0 + GROUP_W])
        if rope:
            t = (t * cos + pltpu.roll(t, GROUP_W - ROT_HALF, 1) * sin_lo
                 + pltpu.roll(t, ROT_HALF, 1) * sin_hi)
        return t

    a_refs = (a0_ref, a1_ref, a2_ref)
    for g, (_, dil) in enumerate(DIL_PAIRS):
        for t in range(3):
            res = proj(t * WIDTH_A + g * GROUP_W, rope=t < 2)
            if dil == 1:
                a_refs[g][t, 0, 0] = res.astype(BF16)
            else:
                res_ref[0] = res[:, :LANES]
                res_ref[1] = res[:, LANES:]
                n = TM // dil
                for r in range(dil):
                    piece = jnp.concatenate(
                        [res_ref[0, pl.ds(r, n, stride=dil), :],
                         res_ref[1, pl.ds(r, n, stride=dil), :]], axis=-1)
                    a_refs[g][t, 0, r] = piece.astype(BF16)
    for t in range(3):
        for cc in range(WIDTH_B // GROUP_W):
            lo = cc * GROUP_W
            res = proj(3 * WIDTH_A + t * WIDTH_B + lo, rope=t < 2)
            if t == 0:
                res = res * MOBA_Q_SCALE
            b_ref[t, :, lo:lo + GROUP_W] = res.astype(BF16)
            if t == 1:
                for blk in range(TM // MOBA_BLOCK):
                    km_ref[0, blk:blk + 1, lo:lo + GROUP_W] = jnp.mean(
                        res[blk * MOBA_BLOCK:(blk + 1) * MOBA_BLOCK], axis=0, keepdims=True)


def _qkv_call(layer, x, g, w, cos, sin_lo, sin_hi, batch, seq):
    tokens = x.shape[0]
    tiles_per_seq = seq // TM
    tab_spec = pl.BlockSpec((TM, LANES), lambda i: (i, 0))

    def a_spec(dil):
        return pl.BlockSpec((3, 1, dil, TM // dil, GROUP_W),
                            lambda i: (0, i // tiles_per_seq, 0, i % tiles_per_seq, 0))

    out_shape = [jax.ShapeDtypeStruct((3, batch, dil, seq // dil, GROUP_W), BF16) for _, dil in DIL_PAIRS]
    out_shape += [jax.ShapeDtypeStruct((3, tokens, WIDTH_B), BF16),
                  jax.ShapeDtypeStruct((tokens // TM, TM // MOBA_BLOCK, WIDTH_B), F32)]
    out_specs = [a_spec(dil) for _, dil in DIL_PAIRS]
    out_specs += [pl.BlockSpec((3, TM, WIDTH_B), lambda i: (0, i, 0)),
                  pl.BlockSpec((1, TM // MOBA_BLOCK, WIDTH_B), lambda i: (i, 0, 0))]
    return pl.pallas_call(
        _qkv_body,
        grid=(tokens // TM,),
        in_specs=[pl.BlockSpec((TM, D_MODEL), lambda i: (i, 0)),
                  pl.BlockSpec((None, 1, D_MODEL), lambda i: (layer, 0, 0)),
                  pl.BlockSpec((None, D_MODEL, QKV_COLS), lambda i: (layer, 0, 0)),
                  tab_spec, tab_spec, tab_spec],
        out_specs=out_specs,
        out_shape=out_shape,
        scratch_shapes=[pltpu.VMEM((2, TM, LANES), F32)],
        compiler_params=_params(1),
        name="qkv_proj",
    )(x, g, w, cos, sin_lo, sin_hi)


def _head_masks(rows):
    lane = lax.broadcasted_iota(jnp.int32, (rows, GROUP_W), 1)
    return [(lane >= h * HEAD_DIM) & (lane < (h + 1) * HEAD_DIM) for h in range(HEADS_PER_GROUP)]


def _dilated_body(a0_ref, a1_ref, a2_ref, ya_ref, o_scr, lse_scr, qmask_scr, bias_scr, *s_slots, seq):
    hmask = _head_masks(BAND)
    row = lax.broadcasted_iota(jnp.int32, (BAND, 2 * BAND), 0)
    col = lax.broadcasted_iota(jnp.int32, (BAND, 2 * BAND), 1)
    band = ((col < BAND) & (col >= row)) | ((col >= BAND) & (col - BAND <= row))
    bias_scr[1] = jnp.where(band, 0.0, NEG)
    bias_scr[0] = jnp.where(band & (col >= BAND), 0.0, NEG)
    for h in range(HEADS_PER_GROUP):
        qmask_scr[h] = jnp.where(hmask[h], HEAD_DIM ** -0.5, 0.0).astype(BF16)

    a_refs = (a0_ref, a1_ref, a2_ref)
    units = [(g, r, n) for g, (_, dil) in enumerate(DIL_PAIRS)
             for r in range(dil) for n in range(seq // dil // BAND)]

    def keys_of(unit, t):
        g, r, n = unit
        cur = a_refs[g][t, 0, r, n * BAND:(n + 1) * BAND, :]
        if seq // DIL_PAIRS[g][1] // BAND == 1:
            return cur
        p = max(n - 1, 0)
        return jnp.concatenate([a_refs[g][t, 0, r, p * BAND:(p + 1) * BAND, :], cur], axis=0)

    def scores(unit, slot):
        g, r, n = unit
        q = a_refs[g][0, 0, r, n * BAND:(n + 1) * BAND, :]
        q4 = jnp.concatenate([q * qmask_scr[h] for h in range(HEADS_PER_GROUP)], axis=0)
        k = keys_of(unit, 1)
        nk = k.shape[0]
        bias = bias_scr[min(n, 1)] if nk == 2 * BAND else bias_scr[1, :, BAND:]
        s = _dot_nt(q4, k).reshape(HEADS_PER_GROUP, BAND, nk)
        slot[:, 0:nk] = (s + bias[None]).reshape(HEADS_PER_GROUP * BAND, nk)

    n_slot = len(s_slots)
    for i in range(min(DIL_LOOKAHEAD, len(units))):
        scores(units[i], s_slots[i % n_slot])
    for i, unit in enumerate(units):
        if i + DIL_LOOKAHEAD < len(units):
            scores(units[i + DIL_LOOKAHEAD], s_slots[(i + DIL_LOOKAHEAD) % n_slot])
        g, r, n = unit
        dil = DIL_PAIRS[g][1]
        v = keys_of(unit, 2)
        s = s_slots[i % n_slot][:, 0:v.shape[0]]
        m = jnp.max(s, axis=-1, keepdims=True)
        p = jnp.exp(s - m)
        l = jnp.sum(p, axis=-1, keepdims=True)
        o4 = _dot(p.astype(BF16), v) * (1.0 / l)
        lse4 = jnp.broadcast_to(m + jnp.log(l), (HEADS_PER_GROUP * BAND, GROUP_W))
        o = o4[0:BAND]
        lse = lse4[0:BAND]
        for h in range(1, HEADS_PER_GROUP):
            o = jnp.where(hmask[h], o4[h * BAND:(h + 1) * BAND], o)
            lse = jnp.where(hmask[h], lse4[h * BAND:(h + 1) * BAND], lse)
        start = n * BAND * dil + r
        dst = pl.ds(start, BAND) if dil == 1 else pl.ds(start, BAND, stride=dil)
        for slab in range(GROUP_W // LANES):
            o_scr[g, slab, dst, :] = o[:, slab * LANES:(slab + 1) * LANES]
            lse_scr[g, slab, dst, :] = lse[:, slab * LANES:(slab + 1) * LANES]

    chunk = 256
    for c in range(seq // chunk):
        rows = pl.ds(c * chunk, chunk)
        for slab in range(GROUP_W // LANES):
            lses = [lse_scr[g, slab, rows, :] for g in range(len(DIL_PAIRS))]
            top = jnp.maximum(jnp.maximum(lses[0], lses[1]), lses[2])
            es = [jnp.exp(l - top) for l in lses]
            num = sum(e * o_scr[g, slab, rows, :] for g, e in enumerate(es))
            ya_ref[rows, slab * LANES:(slab + 1) * LANES] = (num / (es[0] + es[1] + es[2])).astype(BF16)


def _dilated_call(a0, a1, a2, batch, seq):
    def a_spec(dil):
        return pl.BlockSpec((3, 1, dil, seq // dil, GROUP_W), lambda b: (0, b, 0, 0, 0))

    n_groups = len(DIL_PAIRS)
    return pl.pallas_call(
        functools.partial(_dilated_body, seq=seq),
        grid=(batch,),
        in_specs=[a_spec(dil) for _, dil in DIL_PAIRS],
        out_specs=pl.BlockSpec((seq, GROUP_W), lambda b: (b, 0)),
        out_shape=jax.ShapeDtypeStruct((batch * seq, GROUP_W), BF16),
        scratch_shapes=[pltpu.VMEM((n_groups, GROUP_W // LANES, seq, LANES), F32),
                        pltpu.VMEM((n_groups, GROUP_W // LANES, seq, LANES), F32),
                        pltpu.VMEM((HEADS_PER_GROUP, BAND, GROUP_W), BF16),
                        pltpu.VMEM((2, BAND, 2 * BAND), F32)]
                       + [pltpu.VMEM((HEADS_PER_GROUP * BAND, 2 * BAND), F32)] * (DIL_LOOKAHEAD + 1),
        compiler_params=_params(1),
        name="dilated_attn",
    )(a0, a1, a2)


def _fold_rows(tiles, op):
    parts = []
    for t in tiles:
        while t.shape[0] > 8:
            half = t.shape[0] // 2
            t = op(t[:half], t[half:])
        parts.append(t)
    while len(parts) > 1:
        parts = [op(parts[i], parts[i + 1]) if i + 1 < len(parts) else parts[i]
                 for i in range(0, len(parts), 2)]
    return parts[0]


def _moba_body(q_ref, k_ref, v_ref, km_ref, yb_ref, vt_scr, qmask_scr, *s_slots, seq):
    nblk = seq // MOBA_BLOCK
    hmask = _head_masks(MOBA_BLOCK)
    key = lax.broadcasted_iota(jnp.int32, (MOBA_BLOCK, MOBA_BLOCK), 0)
    qry = lax.broadcasted_iota(jnp.int32, (MOBA_BLOCK, MOBA_BLOCK), 1)
    causal = key <= qry
    km = km_ref[...]
    km_hi = km.astype(BF16)
    km_lo = (km - km_hi.astype(F32)).astype(BF16)
    blk_id = lax.broadcasted_iota(jnp.int32, (nblk, MOBA_BLOCK), 0)
    for h in range(HEADS_PER_GROUP):
        qmask_scr[h] = jnp.where(hmask[h], 1.0, 0.0).astype(BF16)
    heads = range(HEADS_PER_GROUP)
    for h in heads:
        vt_scr[h * VT_ROWS + HEAD_DIM:(h + 1) * VT_ROWS, :] = jnp.ones((VT_ROWS - HEAD_DIM, seq), BF16)
    for j in range(nblk):
        blk = slice(j * MOBA_BLOCK, (j + 1) * MOBA_BLOCK)
        vt = v_ref[blk, :].T
        for h in heads:
            vt_scr[h * VT_ROWS:h * VT_ROWS + HEAD_DIM, blk] = vt[h * HEAD_DIM:(h + 1) * HEAD_DIM]

    units = [(j, n, h) for j in range(nblk) for n in [j] + list(range(j)) for h in heads]
    qh, pen = {}, {}

    def scores(unit):
        j, n, h = unit
        if (j, h) not in qh:
            qh[j, h] = q_ref[j * MOBA_BLOCK:(j + 1) * MOBA_BLOCK, :] * qmask_scr[h]
        return _dot_nt(k_ref[n * MOBA_BLOCK:(n + 1) * MOBA_BLOCK, :], qh[j, h])

    def penalty(j, h):
        if (j, h) not in pen:
            gate = _dot_nt(km_hi, qh[j, h]) + _dot_nt(km_lo, qh[j, h])
            beaten = jnp.zeros((nblk, MOBA_BLOCK), F32)
            for m in range(j):
                gm = gate[m:m + 1, :]
                beats = (gm > gate) | ((gm == gate) & (blk_id > m))
                beaten = beaten + jnp.where(beats, 1.0, 0.0)
            pen[j, h] = jnp.where(beaten >= MOBA_TOPK, NEG, 0.0)
        return pen[j, h]

    nslot = len(s_slots)
    for i in range(min(MOBA_LOOKAHEAD, len(units))):
        s_slots[i % nslot][...] = scores(units[i])
    m_run, l_run, acc = {}, {}, {}
    for i, (j, n, h) in enumerate(units):
        if i + MOBA_LOOKAHEAD < len(units):
            s_slots[(i + MOBA_LOOKAHEAD) % nslot][...] = scores(units[i + MOBA_LOOKAHEAD])
        s = s_slots[i % nslot][...]
        if n == j and h == 0 and j + 1 < nblk and j + 1 > MOBA_TOPK:
            for hh in heads:
                if (j + 1, hh) not in qh:
                    qh[j + 1, hh] = q_ref[(j + 1) * MOBA_BLOCK:(j + 2) * MOBA_BLOCK, :] * qmask_scr[hh]
                penalty(j + 1, hh)
        if n == j:
            s = jnp.where(causal, s, NEG)
        elif j > MOBA_TOPK:
            s = s + penalty(j, h)[n:n + 1, :]
        m_blk = jnp.max(_fold_rows([s], jnp.maximum), axis=0, keepdims=True)
        m_new = m_blk if n == j else jnp.maximum(m_run[h], m_blk)
        p = jnp.exp2(s - m_new).astype(BF16)
        vt = vt_scr[h * VT_ROWS:(h + 1) * VT_ROWS, n * MOBA_BLOCK:(n + 1) * MOBA_BLOCK]
        pv = _dot(vt, p)
        acc[h] = pv if n == j else jnp.exp2(m_run[h] - m_new) * acc[h] + pv
        m_run[h] = m_new
        if n == max(j - 1, 0) and h == HEADS_PER_GROUP - 1:
            out_t = jnp.concatenate(
                [acc[hh][:HEAD_DIM] * (1.0 / acc[hh][HEAD_DIM:HEAD_DIM + 1]) for hh in heads], axis=0)
            yb_ref[j * MOBA_BLOCK:(j + 1) * MOBA_BLOCK, :] = out_t.T.astype(BF16)


def _moba_call(bqkv, kmean, batch, seq):
    def qkv_spec(t):
        return pl.BlockSpec((None, seq, GROUP_W), lambda b, hg: (t, b, hg))

    nblk = seq // MOBA_BLOCK
    return pl.pallas_call(
        functools.partial(_moba_body, seq=seq),
        grid=(batch, WIDTH_B // GROUP_W),
        in_specs=[qkv_spec(0), qkv_spec(1), qkv_spec(2),
                  pl.BlockSpec((None, nblk, GROUP_W), lambda b, hg: (b, 0, hg))],
        out_specs=pl.BlockSpec((seq, GROUP_W), lambda b, hg: (b, hg)),
        out_shape=jax.ShapeDtypeStruct((batch * seq, WIDTH_B), BF16),
        scratch_shapes=[pltpu.VMEM((HEADS_PER_GROUP * VT_ROWS, seq), BF16),
                        pltpu.VMEM((HEADS_PER_GROUP, MOBA_BLOCK, GROUP_W), BF16)]
                       + [pltpu.VMEM((MOBA_BLOCK, MOBA_BLOCK), F32)] * (MOBA_LOOKAHEAD + 1),
        compiler_params=_params(2),
        name="moba_attn",
    )(bqkv, bqkv, bqkv, kmean)


def _mix_body(x_ref, ya_ref, yb_ref, gpre_ref, wg_ref, bg_ref, wpa_ref, wpb_ref, wo_ref, gpost_ref,
              out_ref):
    x = x_ref[...]
    h = _rms(x, gpre_ref[...]).astype(BF16)
    bias = bg_ref[...]
    gate_a = jax.nn.sigmoid(_dot(h, wg_ref[:, QKV_COLS:QKV_COLS + D_MODEL]) + bias[:, :D_MODEL])
    merged = gate_a * _dot(ya_ref[...], wpa_ref[...])
    gate_b = jax.nn.sigmoid(_dot(h, wg_ref[:, QKV_COLS + D_MODEL:]) + bias[:, D_MODEL:])
    merged = merged + gate_b * _dot(yb_ref[...], wpb_ref[...])
    out_ref[...] = x + _rms(_dot(merged.astype(BF16), wo_ref[...]), gpost_ref[...])


def _mix_call(layer, x, ya, yb, gpre, wg, bg, wpa, wpb, wo, gpost):
    tokens = x.shape[0]

    def full(shape):
        return pl.BlockSpec((None,) + shape, lambda i: (layer,) + (0,) * len(shape))

    def rows(width):
        return pl.BlockSpec((TM, width), lambda i: (i, 0))

    return pl.pallas_call(
        _mix_body,
        grid=(tokens // TM,),
        in_specs=[rows(D_MODEL), rows(GROUP_W), rows(WIDTH_B), full((1, D_MODEL)),
                  full((D_MODEL, D_IN)), full((1, 2 * D_MODEL)),
                  full((GROUP_W, D_MODEL)), full((WIDTH_B, D_MODEL)), full((D_MODEL, D_MODEL)),
                  full((1, D_MODEL))],
        out_specs=rows(D_MODEL),
        out_shape=jax.ShapeDtypeStruct((tokens, D_MODEL), F32),
        compiler_params=_params(1),
        name="post_mixer",
    )(x, ya, yb, gpre, wg, bg, wpa, wpb, wo, gpost)


def _ffn_body(x_ref, halo_ref, gpre_ref, wug_ref, wuv_ref, cwg_ref, cwv_ref, cbg_ref, cbv_ref,
              wd_ref, gpost_ref, out_ref, h_scr, acc_scr, act_scr, slab_scr, *u_slots, tiles_per_seq):
    i = pl.program_id(0)
    c = pl.program_id(1)
    n_slab = D_MODEL // LANES

    @pl.when(c == 0)
    def _():
        g = gpre_ref[...]
        keep = jnp.where(i % tiles_per_seq == 0, 0.0, 1.0)
        h_scr[0:HALO, :] = (_rms(halo_ref[...], g) * keep).astype(BF16)
        h = _rms(x_ref[...], g)
        for run in range(SUBLANES):
            for sl in range(n_slab):
                slab_scr[sl, pl.ds(run, FFN_RUN, stride=SUBLANES), :] = h[run * FFN_RUN:(run + 1) * FFN_RUN,
                                                                          sl * LANES:(sl + 1) * LANES]
        h_scr[HALO:, :] = jnp.concatenate([slab_scr[sl] for sl in range(n_slab)], axis=-1).astype(BF16)
        acc_scr[...] = jnp.zeros_like(acc_scr)

    h = h_scr[...]
    n_sub = FF_CHUNK // FF_SUB

    def up(s):
        cols = slice(s * FF_SUB, (s + 1) * FF_SUB)
        u_slots[s % 2][0] = _dot(h, wug_ref[:, cols])
        u_slots[s % 2][1] = _dot(h, wuv_ref[:, cols])

    first_sublane = lax.broadcasted_iota(jnp.int32, (SUBLANES, FF_SUB), 0) == 0

    def conv(u, cw_ref, cb_ref, cols):
        cw = cw_ref[:, cols]

        def before_first(back):
            last = u[HALO + TM_FFN - back * SUBLANES:HALO + TM_FFN - (back - 1) * SUBLANES]
            return jnp.where(first_sublane, u[HALO - back:HALO - back + 1], pltpu.roll(last, 1, 0))

        ext = jnp.concatenate([before_first(2), before_first(1), u[HALO:]], axis=0)
        return (ext[0:TM_FFN] * cw[0:1] + ext[SUBLANES:SUBLANES + TM_FFN] * cw[1:2]
                + ext[2 * SUBLANES:] * cw[2:3] + cb_ref[:, cols])

    def down(half):
        rows = slice(half * (FF_CHUNK // 2), (half + 1) * (FF_CHUNK // 2))
        acc_scr[...] += _dot(act_scr[:, rows], wd_ref[rows, :])

    up(0)
    for s in range(n_sub):
        if s + 1 < n_sub:
            up(s + 1)
        if s == n_sub // 2 + 1:
            down(0)
        cols = slice(s * FF_SUB, (s + 1) * FF_SUB)
        gate = conv(u_slots[s % 2][0], cwg_ref, cbg_ref, cols)
        val = conv(u_slots[s % 2][1], cwv_ref, cbv_ref, cols)
        act_scr[:, cols] = (jax.nn.gelu(gate, approximate=True) * val).astype(BF16)
    down(1)

    @pl.when(c == pl.num_programs(1) - 1)
    def _():
        y = _rms(acc_scr[...], gpost_ref[...])
        for sl in range(n_slab):
            slab_scr[sl] = y[:, sl * LANES:(sl + 1) * LANES]
        for run in range(SUBLANES):
            rows = slice(run * FFN_RUN, (run + 1) * FFN_RUN)
            y_run = jnp.concatenate([slab_scr[sl, pl.ds(run, FFN_RUN, stride=SUBLANES), :]
                                     for sl in range(n_slab)], axis=-1)
            out_ref[rows, :] = x_ref[rows, :] + y_run


def _ffn_call(layer, x, gpre, wu, cw, cb, wd, gpost, seq):
    tokens = x.shape[0]
    n_chunks = D_FF // FF_CHUNK
    halo_per_tile = TM_FFN // HALO
    vec = pl.BlockSpec((None, 1, D_MODEL), lambda i, c: (layer, 0, 0))
    return pl.pallas_call(
        functools.partial(_ffn_body, tiles_per_seq=seq // TM_FFN),
        grid=(tokens // TM_FFN, n_chunks),
        in_specs=[pl.BlockSpec((TM_FFN, D_MODEL), lambda i, c: (i, 0)),
                  pl.BlockSpec((HALO, D_MODEL), lambda i, c: (jnp.maximum(i * halo_per_tile - 1, 0), 0)),
                  vec,
                  pl.BlockSpec((None, D_MODEL, FF_CHUNK), lambda i, c: (layer, 0, c)),
                  pl.BlockSpec((None, D_MODEL, FF_CHUNK), lambda i, c: (layer, 0, n_chunks + c)),
                  pl.BlockSpec((None, 3, FF_CHUNK), lambda i, c: (layer, 0, c)),
                  pl.BlockSpec((None, 3, FF_CHUNK), lambda i, c: (layer, 0, n_chunks + c)),
                  pl.BlockSpec((None, 1, FF_CHUNK), lambda i, c: (layer, 0, c)),
                  pl.BlockSpec((None, 1, FF_CHUNK), lambda i, c: (layer, 0, n_chunks + c)),
                  pl.BlockSpec((None, FF_CHUNK, D_MODEL), lambda i, c: (layer, c, 0)),
                  vec],
        out_specs=pl.BlockSpec((TM_FFN, D_MODEL), lambda i, c: (i, 0)),
        out_shape=jax.ShapeDtypeStruct((tokens, D_MODEL), F32),
        scratch_shapes=[pltpu.VMEM((HALO + TM_FFN, D_MODEL), BF16), pltpu.VMEM((TM_FFN, D_MODEL), F32),
                        pltpu.VMEM((TM_FFN, FF_CHUNK), BF16),
                        pltpu.VMEM((D_MODEL // LANES, TM_FFN, LANES), F32)]
                       + [pltpu.VMEM((2, HALO + TM_FFN, FF_SUB), F32)] * 2,
        compiler_params=_params(2),
        name="conv_ffn",
    )(x, x, gpre, wu, wu, cw, cw, cb, cb, wd, gpost)


def _rope_tables(positions):
    inv_freq = ROPE_THETA ** (-jnp.arange(0, ROT_DIM, 2, dtype=F32) / ROT_DIM)
    lane = jnp.arange(LANES, dtype=jnp.int32) % HEAD_DIM
    ang = positions.astype(F32).reshape(-1, 1) * inv_freq[lane % ROT_HALF][None, :]
    cos = jnp.where(lane < ROT_DIM, jnp.cos(ang), 1.0)
    sin = jnp.sin(ang)
    return (cos, jnp.where(lane < ROT_HALF, -sin, 0.0),
            jnp.where((lane >= ROT_HALF) & (lane < ROT_DIM), sin, 0.0))


def kernel(x, positions, norm_pre_mix, w_in, b_gate, w_proj_a, w_proj_b, w_out, norm_post_mix,
           norm_pre_ffn, w_up, conv_w, conv_b, w_down, norm_post_ffn):
    batch, seq, _ = x.shape
    depth = w_in.shape[0]
    assert seq % TM == 0 and seq % TM_FFN == 0 and TM % MOBA_BLOCK == 0 and seq // DIL_PAIRS[-1][1] == BAND
    cos, sin_lo, sin_hi = _rope_tables(positions)
    w_i = w_in.astype(BF16)
    w_pa, w_pb, w_o = w_proj_a.astype(BF16), w_proj_b.astype(BF16), w_out.astype(BF16)
    w_u, w_d = w_up.astype(BF16), w_down.astype(BF16)
    rows = lambda p: p.reshape(depth, 1, -1)
    g_pre_mix, g_post_mix = rows(norm_pre_mix), rows(norm_post_mix)
    g_pre_ffn, g_post_ffn = rows(norm_pre_ffn), rows(norm_post_ffn)
    bias_gate, bias_conv = rows(b_gate), rows(conv_b)

    xt = x.reshape(batch * seq, D_MODEL)
    for l in range(depth):
        a0, a1, a2, bqkv, kmean = _qkv_call(l, xt, g_pre_mix, w_i, cos, sin_lo, sin_hi, batch, seq)
        ya = _dilated_call(a0, a1, a2, batch, seq)
        yb = _moba_call(bqkv, kmean.reshape(batch, seq // MOBA_BLOCK, WIDTH_B), batch, seq)
        xt = _mix_call(l, xt, ya, yb, g_pre_mix, w_i, bias_gate, w_pa, w_pb, w_o, g_post_mix)
        xt = _ffn_call(l, xt, g_pre_ffn, w_u, conv_w, bias_conv, w_d, g_post_ffn, seq)
    return xt.reshape(batch, seq, D_MODEL)
```

```python
import functools

import jax
import jax.numpy as jnp
from jax import lax
from jax.experimental import pallas as pl
from jax.experimental.pallas import tpu as pltpu

F32 = jnp.float32
BF16 = jnp.bfloat16

D_MODEL = 1024
HEAD_DIM = 64
DIL_PAIRS = ((128, 1), (512, 4), (2048, 16))
BAND = 128
HEADS_PER_GROUP = 4
GROUP_W = HEADS_PER_GROUP * HEAD_DIM
WIDTH_A = 3 * GROUP_W
WIDTH_B = 512
MOBA_BLOCK = 256
MOBA_TOPK = 3
ROPE_THETA = 500000.0
ROT_DIM = HEAD_DIM // 4
ROT_HALF = ROT_DIM // 2
D_FF = 4 * D_MODEL
RMS_EPS = 1e-6
QKV_COLS = 3 * WIDTH_A + 3 * WIDTH_B
D_IN = QKV_COLS + 2 * D_MODEL
LANES = 128
NEG = -1e30

TM = 512
TM_FFN = 1024
SUBLANES = 8
FFN_RUN = TM_FFN // SUBLANES
FF_CHUNK = 1024
FF_SUB = 256
MOBA_Q_SCALE = HEAD_DIM ** -0.5 * 1.4426950408889634
MOBA_LOOKAHEAD = 7
VT_ROWS = HEAD_DIM + 16
DIL_LOOKAHEAD = 2
HALO = 16
VMEM_LIMIT = 56 * 1024 * 1024


def _rms(x, g):
    return x * lax.rsqrt(jnp.mean(x * x, axis=-1, keepdims=True) + RMS_EPS) * g


def _dot(a, b):
    return jnp.dot(a, b, preferred_element_type=F32)


def _dot_nt(a, b):
    return lax.dot_general(a, b, (((1,), (1,)), ((), ())), preferred_element_type=F32)


def _params(n_grid):
    return pltpu.CompilerParams(dimension_semantics=("arbitrary",) * n_grid,
                                vmem_limit_bytes=VMEM_LIMIT)


def _qkv_body(x_ref, g_ref, w_ref, c_ref, s1_ref, s2_ref,
              a0_ref, a1_ref, a2_ref, b_ref, km_ref, res_ref):
    h = _rms(x_ref[...], g_ref[...]).astype(BF16)
    cos = jnp.concatenate([c_ref[...]] * 2, axis=-1)
    sin_lo = jnp.concatenate([s1_ref[...]] * 2, axis=-1)
    sin_hi = jnp.concatenate([s2_ref[...]] * 2, axis=-1)

    def proj(c0, rope):
        t = _dot(h, w_ref[:, c0:c0 + GROUP_W])
        if rope:
            t = (t * cos + pltpu.roll(t, GROUP_W - ROT_HALF, 1) * sin_lo
                 + pltpu.roll(t, ROT_HALF, 1) * sin_hi)
        return t

    a_refs = (a0_ref, a1_ref, a2_ref)
    for g, (_, dil) in enumerate(DIL_PAIRS):
        for t in range(3):
            res = proj(t * WIDTH_A + g * GROUP_W, rope=t < 2)
            if dil == 1:
                a_refs[g][t, 0, 0] = res.astype(BF16)
            else:
                res_ref[0] = res[:, :LANES]
                res_ref[1] = res[:, LANES:]
                n = TM // dil
                for r in range(dil):
                    piece = jnp.concatenate(
                        [res_ref[0, pl.ds(r, n, stride=dil), :],
                         res_ref[1, pl.ds(r, n, stride=dil), :]], axis=-1)
                    a_refs[g][t, 0, r] = piece.astype(BF16)
    for t in range(3):
        for cc in range(WIDTH_B // GROUP_W):
            lo = cc * GROUP_W
            res = proj(3 * WIDTH_A + t * WIDTH_B + lo, rope=t < 2)
            if t == 0:
                res = res * MOBA_Q_SCALE
            b_ref[t, :, lo:lo + GROUP_W] = res.astype(BF16)
            if t == 1:
                for blk in range(TM // MOBA_BLOCK):
                    km_ref[0, blk:blk + 1, lo:lo + GROUP_W] = jnp.mean(
                        res[blk * MOBA_BLOCK:(blk + 1) * MOBA_BLOCK], axis=0, keepdims=True)


def _qkv_call(layer, x, g, w, cos, sin_lo, sin_hi, batch, seq):
    tokens = x.shape[0]
    tiles_per_seq = seq // TM
    tab_spec = pl.BlockSpec((TM, LANES), lambda i: (i, 0))

    def a_spec(dil):
        return pl.BlockSpec((3, 1, dil, TM // dil, GROUP_W),
                            lambda i: (0, i // tiles_per_seq, 0, i % tiles_per_seq, 0))

    out_shape = [jax.ShapeDtypeStruct((3, batch, dil, seq // dil, GROUP_W), BF16) for _, dil in DIL_PAIRS]
    out_shape += [jax.ShapeDtypeStruct((3, tokens, WIDTH_B), BF16),
                  jax.ShapeDtypeStruct((tokens // TM, TM // MOBA_BLOCK, WIDTH_B), F32)]
    out_specs = [a_spec(dil) for _, dil in DIL_PAIRS]
    out_specs += [pl.BlockSpec((3, TM, WIDTH_B), lambda i: (0, i, 0)),
                  pl.BlockSpec((1, TM // MOBA_BLOCK, WIDTH_B), lambda i: (i, 0, 0))]
    return pl.pallas_call(
        _qkv_body,
        grid=(tokens // TM,),
        in_specs=[pl.BlockSpec((TM, D_MODEL), lambda i: (i, 0)),
                  pl.BlockSpec((None, 1, D_MODEL), lambda i: (layer, 0, 0)),
                  pl.BlockSpec((None, D_MODEL, QKV_COLS), lambda i: (layer, 0, 0)),
                  tab_spec, tab_spec, tab_spec],
        out_specs=out_specs,
        out_shape=out_shape,
        scratch_shapes=[pltpu.VMEM((2, TM, LANES), F32)],
        compiler_params=_params(1),
        name="qkv_proj",
    )(x, g, w, cos, sin_lo, sin_hi)


def _head_masks(rows):
    lane = lax.broadcasted_iota(jnp.int32, (rows, GROUP_W), 1)
    return [(lane >= h * HEAD_DIM) & (lane < (h + 1) * HEAD_DIM) for h in range(HEADS_PER_GROUP)]


def _dilated_body(a0_ref, a1_ref, a2_ref, ya_ref, o_scr, lse_scr, qmask_scr, bias_scr, *s_slots, seq):
    hmask = _head_masks(BAND)
    row = lax.broadcasted_iota(jnp.int32, (BAND, 2 * BAND), 0)
    col = lax.broadcasted_iota(jnp.int32, (BAND, 2 * BAND), 1)
    band = ((col < BAND) & (col >= row)) | ((col >= BAND) & (col - BAND <= row))
    bias_scr[1] = jnp.where(band, 0.0, NEG)
    bias_scr[0] = jnp.where(band & (col >= BAND), 0.0, NEG)
    for h in range(HEADS_PER_GROUP):
        qmask_scr[h] = jnp.where(hmask[h], HEAD_DIM ** -0.5, 0.0).astype(BF16)

    a_refs = (a0_ref, a1_ref, a2_ref)
    units = [(g, r, n) for g, (_, dil) in enumerate(DIL_PAIRS)
             for r in range(dil) for n in range(seq // dil // BAND)]

    def keys_of(unit, t):
        g, r, n = unit
        cur = a_refs[g][t, 0, r, n * BAND:(n + 1) * BAND, :]
        if seq // DIL_PAIRS[g][1] // BAND == 1:
            return cur
        p = max(n - 1, 0)
        return jnp.concatenate([a_refs[g][t, 0, r, p * BAND:(p + 1) * BAND, :], cur], axis=0)

    def scores(unit, slot):
        g, r, n = unit
        q = a_refs[g][0, 0, r, n * BAND:(n + 1) * BAND, :]
        q4 = jnp.concatenate([q * qmask_scr[h] for h in range(HEADS_PER_GROUP)], axis=0)
        k = keys_of(unit, 1)
        nk = k.shape[0]
        bias = bias_scr[min(n, 1)] if nk == 2 * BAND else bias_scr[1, :, BAND:]
        s = _dot_nt(q4, k).reshape(HEADS_PER_GROUP, BAND, nk)
        slot[:, 0:nk] = (s + bias[None]).reshape(HEADS_PER_GROUP * BAND, nk)

    n_slot = len(s_slots)
    for i in range(min(DIL_LOOKAHEAD, len(units))):
        scores(units[i], s_slots[i % n_slot])
    for i, unit in enumerate(units):
        if i + DIL_LOOKAHEAD < len(units):
            scores(units[i + DIL_LOOKAHEAD], s_slots[(i + DIL_LOOKAHEAD) % n_slot])
        g, r, n = unit
        dil = DIL_PAIRS[g][1]
        v = keys_of(unit, 2)
        s = s_slots[i % n_slot][:, 0:v.shape[0]]
        m = jnp.max(s, axis=-1, keepdims=True)
        p = jnp.exp(s - m)
        l = jnp.sum(p, axis=-1, keepdims=True)
        o4 = _dot(p.astype(BF16), v) * (1.0 / l)
        lse4 = jnp.broadcast_to(m + jnp.log(l), (HEADS_PER_GROUP * BAND, GROUP_W))
        o = o4[0:BAND]
        lse = lse4[0:BAND]
        for h in range(1, HEADS_PER_GROUP):
            o = jnp.where(hmask[h], o4[h * BAND:(h + 1) * BAND], o)
            lse = jnp.where(hmask[h], lse4[h * BAND:(h + 1) * BAND], lse)
        start = n * BAND * dil + r
        dst = pl.ds(start, BAND) if dil == 1 else pl.ds(start, BAND, stride=dil)
        for slab in range(GROUP_W // LANES):
            o_scr[g, slab, dst, :] = o[:, slab * LANES:(slab + 1) * LANES]
            lse_scr[g, slab, dst, :] = lse[:, slab * LANES:(slab + 1) * LANES]

    chunk = 256
    for c in range(seq // chunk):
        rows = pl.ds(c * chunk, chunk)
        for slab in range(GROUP_W // LANES):
            lses = [lse_scr[g, slab, rows, :] for g in range(len(DIL_PAIRS))]
            top = jnp.maximum(jnp.maximum(lses[0], lses[1]), lses[2])
            es = [jnp.exp(l - top) for l in lses]
            num = sum(e * o_scr[g, slab, rows, :] for g, e in enumerate(es))
            ya_ref[rows, slab * LANES:(slab + 1) * LANES] = (num / (es[0] + es[1] + es[2])).astype(BF16)


def _dilated_call(a0, a1, a2, batch, seq):
    def a_spec(dil):
        return pl.BlockSpec((3, 1, dil, seq // dil, GROUP_W), lambda b: (0, b, 0, 0, 0))

    n_groups = len(DIL_PAIRS)
    return pl.pallas_call(
        functools.partial(_dilated_body, seq=seq),
        grid=(batch,),
        in_specs=[a_spec(dil) for _, dil in DIL_PAIRS],
        out_specs=pl.BlockSpec((seq, GROUP_W), lambda b: (b, 0)),
        out_shape=jax.ShapeDtypeStruct((batch * seq, GROUP_W), BF16),
        scratch_shapes=[pltpu.VMEM((n_groups, GROUP_W // LANES, seq, LANES), F32),
                        pltpu.VMEM((n_groups, GROUP_W // LANES, seq, LANES), F32),
                        pltpu.VMEM((HEADS_PER_GROUP, BAND, GROUP_W), BF16),
                        pltpu.VMEM((2, BAND, 2 * BAND), F32)]
                       + [pltpu.VMEM((HEADS_PER_GROUP * BAND, 2 * BAND), F32)] * (DIL_LOOKAHEAD + 1),
        compiler_params=_params(1),
        name="dilated_attn",
    )(a0, a1, a2)


def _fold_rows(tiles, op):
    parts = []
    for t in tiles:
        while t.shape[0] > 8:
            half = t.shape[0] // 2
            t = op(t[:half], t[half:])
        parts.append(t)
    while len(parts) > 1:
        parts = [op(parts[i], parts[i + 1]) if i + 1 < len(parts) else parts[i]
                 for i in range(0, len(parts), 2)]
    return parts[0]


def _moba_body(q_ref, k_ref, v_ref, km_ref, yb_ref, vt_scr, qmask_scr, *s_slots, seq):
    nblk = seq // MOBA_BLOCK
    hmask = _head_masks(MOBA_BLOCK)
    key = lax.broadcasted_iota(jnp.int32, (MOBA_BLOCK, MOBA_BLOCK), 0)
    qry = lax.broadcasted_iota(jnp.int32, (MOBA_BLOCK, MOBA_BLOCK), 1)
    causal = key <= qry
    km = km_ref[...]
    km_hi = km.astype(BF16)
    km_lo = (km - km_hi.astype(F32)).astype(BF16)
    blk_id = lax.broadcasted_iota(jnp.int32, (nblk, MOBA_BLOCK), 0)
    for h in range(HEADS_PER_GROUP):
        qmask_scr[h] = jnp.where(hmask[h], 1.0, 0.0).astype(BF16)
    heads = range(HEADS_PER_GROUP)
    for h in heads:
        vt_scr[h * VT_ROWS + HEAD_DIM:(h + 1) * VT_ROWS, :] = jnp.ones((VT_ROWS - HEAD_DIM, seq), BF16)
    for j in range(nblk):
        blk = slice(j * MOBA_BLOCK, (j + 1) * MOBA_BLOCK)
        vt = v_ref[blk, :].T
        for h in heads:
            vt_scr[h * VT_ROWS:h * VT_ROWS + HEAD_DIM, blk] = vt[h * HEAD_DIM:(h + 1) * HEAD_DIM]

    units = [(j, n, h) for j in range(nblk) for n in [j] + list(range(j)) for h in heads]
    qh, pen = {}, {}

    def scores(unit):
        j, n, h = unit
        if (j, h) not in qh:
            qh[j, h] = q_ref[j * MOBA_BLOCK:(j + 1) * MOBA_BLOCK, :] * qmask_scr[h]
        return _dot_nt(k_ref[n * MOBA_BLOCK:(n + 1) * MOBA_BLOCK, :], qh[j, h])

    def penalty(j, h):
        if (j, h) not in pen:
            gate = _dot_nt(km_hi, qh[j, h]) + _dot_nt(km_lo, qh[j, h])
            beaten = jnp.zeros((nblk, MOBA_BLOCK), F32)
            for m in range(j):
                gm = gate[m:m + 1, :]
                beats = (gm > gate) | ((gm == gate) & (blk_id > m))
                beaten = beaten + jnp.where(beats, 1.0, 0.0)
            pen[j, h] = jnp.where(beaten >= MOBA_TOPK, NEG, 0.0)
        return pen[j, h]

    nslot = len(s_slots)
    for i in range(min(MOBA_LOOKAHEAD, len(units))):
        s_slots[i % nslot][...] = scores(units[i])
    m_run, l_run, acc = {}, {}, {}
    for i, (j, n, h) in enumerate(units):
        if i + MOBA_LOOKAHEAD < len(units):
            s_slots[(i + MOBA_LOOKAHEAD) % nslot][...] = scores(units[i + MOBA_LOOKAHEAD])
        s = s_slots[i % nslot][...]
        if n == j and h == 0 and j + 1 < nblk and j + 1 > MOBA_TOPK:
            for hh in heads:
                if (j + 1, hh) not in qh:
                    qh[j + 1, hh] = q_ref[(j + 1) * MOBA_BLOCK:(j + 2) * MOBA_BLOCK, :] * qmask_scr[hh]
                penalty(j + 1, hh)
        if n == j:
            s = jnp.where(causal, s, NEG)
        elif j > MOBA_TOPK:
            s = s + penalty(j, h)[n:n + 1, :]
        m_blk = jnp.max(_fold_rows([s], jnp.maximum), axis=0, keepdims=True)
        m_new = m_blk if n == j else jnp.maximum(m_run[h], m_blk)
        p = jnp.exp2(s - m_new).astype(BF16)
        vt = vt_scr[h * VT_ROWS:(h + 1) * VT_ROWS, n * MOBA_BLOCK:(n + 1) * MOBA_BLOCK]
        pv = _dot(vt, p)
        acc[h] = pv if n == j else jnp.exp2(m_run[h] - m_new) * acc[h] + pv
        m_run[h] = m_new
        if n == max(j - 1, 0) and h == HEADS_PER_GROUP - 1:
            out_t = jnp.concatenate(
                [acc[hh][:HEAD_DIM] * (1.0 / acc[hh][HEAD_DIM:HEAD_DIM + 1]) for hh in heads], axis=0)
            yb_ref[j * MOBA_BLOCK:(j + 1) * MOBA_BLOCK, :] = out_t.T.astype(BF16)


def _moba_call(bqkv, kmean, batch, seq):
    def qkv_spec(t):
        return pl.BlockSpec((None, seq, GROUP_W), lambda b, hg: (t, b, hg))

    nblk = seq // MOBA_BLOCK
    return pl.pallas_call(
        functools.partial(_moba_body, seq=seq),
        grid=(batch, WIDTH_B // GROUP_W),
        in_specs=[qkv_spec(0), qkv_spec(1), qkv_spec(2),
                  pl.BlockSpec((None, nblk, GROUP_W), lambda b, hg: (b, 0, hg))],
        out_specs=pl.BlockSpec((seq, GROUP_W), lambda b, hg: (b, hg)),
        out_shape=jax.ShapeDtypeStruct((batch * seq, WIDTH_B), BF16),
        scratch_shapes=[pltpu.VMEM((HEADS_PER_GROUP * VT_ROWS, seq), BF16),
                        pltpu.VMEM((HEADS_PER_GROUP, MOBA_BLOCK, GROUP_W), BF16)]
                       + [pltpu.VMEM((MOBA_BLOCK, MOBA_BLOCK), F32)] * (MOBA_LOOKAHEAD + 1),
        compiler_params=_params(2),
        name="moba_attn",
    )(bqkv, bqkv, bqkv, kmean)


def _mix_body(x_ref, ya_ref, yb_ref, gpre_ref, wg_ref, bg_ref, wpa_ref, wpb_ref, wo_ref, gpost_ref,
              out_ref):
    x = x_ref[...]
    h = _rms(x, gpre_ref[...]).astype(BF16)
    bias = bg_ref[...]
    gate_a = jax.nn.sigmoid(_dot(h, wg_ref[:, QKV_COLS:QKV_COLS + D_MODEL]) + bias[:, :D_MODEL])
    merged = gate_a * _dot(ya_ref[...], wpa_ref[...])
    gate_b = jax.nn.sigmoid(_dot(h, wg_ref[:, QKV_COLS + D_MODEL:]) + bias[:, D_MODEL:])
    merged = merged + gate_b * _dot(yb_ref[...], wpb_ref[...])
    out_ref[...] = x + _rms(_dot(merged.astype(BF16), wo_ref[...]), gpost_ref[...])


def _mix_call(layer, x, ya, yb, gpre, wg, bg, wpa, wpb, wo, gpost):
    tokens = x.shape[0]

    def full(shape):
        return pl.BlockSpec((None,) + shape, lambda i: (layer,) + (0,) * len(shape))

    def rows(width):
        return pl.BlockSpec((TM, width), lambda i: (i, 0))

    return pl.pallas_call(
        _mix_body,
        grid=(tokens // TM,),
        in_specs=[rows(D_MODEL), rows(GROUP_W), rows(WIDTH_B), full((1, D_MODEL)),
                  full((D_MODEL, D_IN)), full((1, 2 * D_MODEL)),
                  full((GROUP_W, D_MODEL)), full((WIDTH_B, D_MODEL)), full((D_MODEL, D_MODEL)),
                  full((1, D_MODEL))],
        out_specs=rows(D_MODEL),
        out_shape=jax.ShapeDtypeStruct((tokens, D_MODEL), F32),
        compiler_params=_params(1),
        name="post_mixer",
    )(x, ya, yb, gpre, wg, bg, wpa, wpb, wo, gpost)


def _ffn_body(x_ref, halo_ref, gpre_ref, wug_ref, wuv_ref, cwg_ref, cwv_ref, cbg_ref, cbv_ref,
              wd_ref, gpost_ref, out_ref, h_scr, acc_scr, act_scr, slab_scr, *, tiles_per_seq):
    i = pl.program_id(0)
    c = pl.program_id(1)
    n_slab = D_MODEL // LANES

    @pl.when(c == 0)
    def _():
        g = gpre_ref[...]
        keep = jnp.where(i % tiles_per_seq == 0, 0.0, 1.0)
        h_scr[0:HALO, :] = (_rms(halo_ref[...], g) * keep).astype(BF16)
        h = _rms(x_ref[...], g)
        for run in range(SUBLANES):
            for sl in range(n_slab):
                slab_scr[sl, pl.ds(run, FFN_RUN, stride=SUBLANES), :] = h[run * FFN_RUN:(run + 1) * FFN_RUN,
                                                                          sl * LANES:(sl + 1) * LANES]
        h_scr[HALO:, :] = jnp.concatenate([slab_scr[sl, 0:TM_FFN, :] for sl in range(n_slab)], axis=-1).astype(BF16)
        acc_scr[...] = jnp.zeros_like(acc_scr)

    h = h_scr[...]
    n_sub = FF_CHUNK // FF_SUB

    def up(s):
        cols = slice(s * FF_SUB, (s + 1) * FF_SUB)
        for t, w_ref in enumerate((wug_ref, wuv_ref)):
            u = _dot(h, w_ref[:, cols])
            for half in range(FF_SUB // LANES):
                slab_scr[(s % 2) * 4 + t * 2 + half] = u[:, half * LANES:(half + 1) * LANES]

    def parked(s, t):
        base = (s % 2) * 4 + t * 2
        return jnp.concatenate([slab_scr[base + half] for half in range(FF_SUB // LANES)], axis=-1)

    first_sublane = lax.broadcasted_iota(jnp.int32, (SUBLANES, FF_SUB), 0) == 0

    def conv(u, cw_ref, cb_ref, cols):
        cw = cw_ref[:, cols]

        def before_first(back):
            last = u[HALO + TM_FFN - back * SUBLANES:HALO + TM_FFN - (back - 1) * SUBLANES]
            return jnp.where(first_sublane, u[HALO - back:HALO - back + 1], pltpu.roll(last, 1, 0))

        ext = jnp.concatenate([before_first(2), before_first(1), u[HALO:]], axis=0)
        return (ext[0:TM_FFN] * cw[0:1] + ext[SUBLANES:SUBLANES + TM_FFN] * cw[1:2]
                + ext[2 * SUBLANES:] * cw[2:3] + cb_ref[:, cols])

    def down(half):
        rows = slice(half * (FF_CHUNK // 2), (half + 1) * (FF_CHUNK // 2))
        acc_scr[...] += _dot(act_scr[:, rows], wd_ref[rows, :])

    up(0)
    for s in range(n_sub):
        if s + 1 < n_sub:
            up(s + 1)
        if s == n_sub // 2 + 1:
            down(0)
        cols = slice(s * FF_SUB, (s + 1) * FF_SUB)
        gate = conv(parked(s, 0), cwg_ref, cbg_ref, cols)
        val = conv(parked(s, 1), cwv_ref, cbv_ref, cols)
        act_scr[:, cols] = (jax.nn.gelu(gate, approximate=True) * val).astype(BF16)
    down(1)

    @pl.when(c == pl.num_programs(1) - 1)
    def _():
        y = _rms(acc_scr[...], gpost_ref[...])
        for sl in range(n_slab):
            slab_scr[sl, 0:TM_FFN, :] = y[:, sl * LANES:(sl + 1) * LANES]
        for run in range(SUBLANES):
            rows = slice(run * FFN_RUN, (run + 1) * FFN_RUN)
            y_run = jnp.concatenate([slab_scr[sl, pl.ds(run, FFN_RUN, stride=SUBLANES), :]
                                     for sl in range(n_slab)], axis=-1)
            out_ref[rows, :] = x_ref[rows, :] + y_run


def _ffn_call(layer, x, gpre, wu, cw, cb, wd, gpost, seq):
    tokens = x.shape[0]
    n_chunks = D_FF // FF_CHUNK
    halo_per_tile = TM_FFN // HALO
    vec = pl.BlockSpec((None, 1, D_MODEL), lambda i, c: (layer, 0, 0))
    return pl.pallas_call(
        functools.partial(_ffn_body, tiles_per_seq=seq // TM_FFN),
        grid=(tokens // TM_FFN, n_chunks),
        in_specs=[pl.BlockSpec((TM_FFN, D_MODEL), lambda i, c: (i, 0)),
                  pl.BlockSpec((HALO, D_MODEL), lambda i, c: (jnp.maximum(i * halo_per_tile - 1, 0), 0)),
                  vec,
                  pl.BlockSpec((None, D_MODEL, FF_CHUNK), lambda i, c: (layer, 0, c)),
                  pl.BlockSpec((None, D_MODEL, FF_CHUNK), lambda i, c: (layer, 0, n_chunks + c)),
                  pl.BlockSpec((None, 3, FF_CHUNK), lambda i, c: (layer, 0, c)),
                  pl.BlockSpec((None, 3, FF_CHUNK), lambda i, c: (layer, 0, n_chunks + c)),
                  pl.BlockSpec((None, 1, FF_CHUNK), lambda i, c: (layer, 0, c)),
                  pl.BlockSpec((None, 1, FF_CHUNK), lambda i, c: (layer, 0, n_chunks + c)),
                  pl.BlockSpec((None, FF_CHUNK, D_MODEL), lambda i, c: (layer, c, 0)),
                  vec],
        out_specs=pl.BlockSpec((TM_FFN, D_MODEL), lambda i, c: (i, 0)),
        out_shape=jax.ShapeDtypeStruct((tokens, D_MODEL), F32),
        scratch_shapes=[pltpu.VMEM((HALO + TM_FFN, D_MODEL), BF16), pltpu.VMEM((TM_FFN, D_MODEL), F32),
                        pltpu.VMEM((TM_FFN, FF_CHUNK), BF16),
                        pltpu.VMEM((D_MODEL // LANES, HALO + TM_FFN, LANES), F32)],
        compiler_params=_params(2),
        name="conv_ffn",
    )(x, x, gpre, wu, wu, cw, cw, cb, cb, wd, gpost)


def _rope_tables(positions):
    inv_freq = ROPE_THETA ** (-jnp.arange(0, ROT_DIM, 2, dtype=F32) / ROT_DIM)
    lane = jnp.arange(LANES, dtype=jnp.int32) % HEAD_DIM
    ang = positions.astype(F32).reshape(-1, 1) * inv_freq[lane % ROT_HALF][None, :]
    cos = jnp.where(lane < ROT_DIM, jnp.cos(ang), 1.0)
    sin = jnp.sin(ang)
    return (cos, jnp.where(lane < ROT_HALF, -sin, 0.0),
            jnp.where((lane >= ROT_HALF) & (lane < ROT_DIM), sin, 0.0))


def kernel(x, positions, norm_pre_mix, w_in, b_gate, w_proj_a, w_proj_b, w_out, norm_post_mix,
           norm_pre_ffn, w_up, conv_w, conv_b, w_down, norm_post_ffn):
    batch, seq, _ = x.shape
    depth = w_in.shape[0]
    assert seq % TM == 0 and seq % TM_FFN == 0 and TM % MOBA_BLOCK == 0 and seq // DIL_PAIRS[-1][1] == BAND
    cos, sin_lo, sin_hi = _rope_tables(positions)
    w_i = w_in.astype(BF16)
    w_pa, w_pb, w_o = w_proj_a.astype(BF16), w_proj_b.astype(BF16), w_out.astype(BF16)
    w_u, w_d = w_up.astype(BF16), w_down.astype(BF16)
    rows = lambda p: p.reshape(depth, 1, -1)
    g_pre_mix, g_post_mix = rows(norm_pre_mix), rows(norm_post_mix)
    g_pre_ffn, g_post_ffn = rows(norm_pre_ffn), rows(norm_post_ffn)
    bias_gate, bias_conv = rows(b_gate), rows(conv_b)

    xt = x.reshape(batch * seq, D_MODEL)
    for l in range(depth):
        a0, a1, a2, bqkv, kmean = _qkv_call(l, xt, g_pre_mix, w_i, cos, sin_lo, sin_hi, batch, seq)
        ya = _dilated_call(a0, a1, a2, batch, seq)
        yb = _moba_call(bqkv, kmean.reshape(batch, seq // MOBA_BLOCK, WIDTH_B), batch, seq)
        xt = _mix_call(l, xt, ya, yb, g_pre_mix, w_i, bias_gate, w_pa, w_pb, w_o, g_post_mix)
        xt = _ffn_call(l, xt, g_pre_ffn, w_u, conv_w, bias_conv, w_d, g_post_ffn, seq)
    return xt.reshape(batch, seq, D_MODEL)
```

```python
import functools

import jax
import jax.numpy as jnp
from jax import lax
from jax.experimental import pallas as pl
from jax.experimental.pallas import tpu as pltpu

F32 = jnp.float32
BF16 = jnp.bfloat16

D_MODEL = 1024
HEAD_DIM = 64
DIL_PAIRS = ((128, 1), (512, 4), (2048, 16))
BAND = 128
HEADS_PER_GROUP = 4
GROUP_W = HEADS_PER_GROUP * HEAD_DIM
WIDTH_A = 3 * GROUP_W
WIDTH_B = 512
MOBA_BLOCK = 256
MOBA_TOPK = 3
ROPE_THETA = 500000.0
ROT_DIM = HEAD_DIM // 4
ROT_HALF = ROT_DIM // 2
D_FF = 4 * D_MODEL
RMS_EPS = 1e-6
QKV_COLS = 3 * WIDTH_A + 3 * WIDTH_B
D_IN = QKV_COLS + 2 * D_MODEL
LANES = 128
NEG = -1e30

TM = 1024
TM_FFN = 1024
SUBLANES = 8
FFN_RUN = TM_FFN // SUBLANES
FF_CHUNK = 1024
FF_SUB = 256
MOBA_Q_SCALE = HEAD_DIM ** -0.5 * 1.4426950408889634
MOBA_LOOKAHEAD = 7
VT_ROWS = HEAD_DIM + 16
DIL_LOOKAHEAD = 2
HALO = 16
VMEM_LIMIT = 56 * 1024 * 1024


def _rms(x, g):
    return x * lax.rsqrt(jnp.mean(x * x, axis=-1, keepdims=True) + RMS_EPS) * g


def _dot(a, b):
    return jnp.dot(a, b, preferred_element_type=F32)


def _dot_nt(a, b):
    return lax.dot_general(a, b, (((1,), (1,)), ((), ())), preferred_element_type=F32)


def _params(n_grid):
    return pltpu.CompilerParams(dimension_semantics=("arbitrary",) * n_grid,
                                vmem_limit_bytes=VMEM_LIMIT)


def _qkv_body(x_ref, g_ref, w_ref, c_ref, s1_ref, s2_ref,
              a0_ref, a1_ref, a2_ref, b_ref, km_ref, res_ref):
    h = _rms(x_ref[...], g_ref[...]).astype(BF16)
    cos = jnp.concatenate([c_ref[...]] * 2, axis=-1)
    sin_lo = jnp.concatenate([s1_ref[...]] * 2, axis=-1)
    sin_hi = jnp.concatenate([s2_ref[...]] * 2, axis=-1)

    def proj(c0, rope):
        t = _dot(h, w_ref[:, c0:c0 + GROUP_W])
        if rope:
            t = (t * cos + pltpu.roll(t, GROUP_W - ROT_HALF, 1) * sin_lo
                 + pltpu.roll(t, ROT_HALF, 1) * sin_hi)
        return t

    a_refs = (a0_ref, a1_ref, a2_ref)
    for g, (_, dil) in enumerate(DIL_PAIRS):
        for t in range(3):
            res = proj(t * WIDTH_A + g * GROUP_W, rope=t < 2)
            if dil == 1:
                a_refs[g][t, 0, 0] = res.astype(BF16)
            else:
                res_ref[0] = res[:, :LANES]
                res_ref[1] = res[:, LANES:]
                n = TM // dil
                for r in range(dil):
                    piece = jnp.concatenate(
                        [res_ref[0, pl.ds(r, n, stride=dil), :],
                         res_ref[1, pl.ds(r, n, stride=dil), :]], axis=-1)
                    a_refs[g][t, 0, r] = piece.astype(BF16)
    for t in range(3):
        for cc in range(WIDTH_B // GROUP_W):
            lo = cc * GROUP_W
            res = proj(3 * WIDTH_A + t * WIDTH_B + lo, rope=t < 2)
            if t == 0:
                res = res * MOBA_Q_SCALE
            b_ref[t, :, lo:lo + GROUP_W] = res.astype(BF16)
            if t == 1:
                for blk in range(TM // MOBA_BLOCK):
                    km_ref[0, blk:blk + 1, lo:lo + GROUP_W] = jnp.mean(
                        res[blk * MOBA_BLOCK:(blk + 1) * MOBA_BLOCK], axis=0, keepdims=True)


def _qkv_call(layer, x, g, w, cos, sin_lo, sin_hi, batch, seq):
    tokens = x.shape[0]
    tiles_per_seq = seq // TM
    tab_spec = pl.BlockSpec((TM, LANES), lambda i: (i, 0))

    def a_spec(dil):
        return pl.BlockSpec((3, 1, dil, TM // dil, GROUP_W),
                            lambda i: (0, i // tiles_per_seq, 0, i % tiles_per_seq, 0))

    out_shape = [jax.ShapeDtypeStruct((3, batch, dil, seq // dil, GROUP_W), BF16) for _, dil in DIL_PAIRS]
    out_shape += [jax.ShapeDtypeStruct((3, tokens, WIDTH_B), BF16),
                  jax.ShapeDtypeStruct((tokens // TM, TM // MOBA_BLOCK, WIDTH_B), F32)]
    out_specs = [a_spec(dil) for _, dil in DIL_PAIRS]
    out_specs += [pl.BlockSpec((3, TM, WIDTH_B), lambda i: (0, i, 0)),
                  pl.BlockSpec((1, TM // MOBA_BLOCK, WIDTH_B), lambda i: (i, 0, 0))]
    return pl.pallas_call(
        _qkv_body,
        grid=(tokens // TM,),
        in_specs=[pl.BlockSpec((TM, D_MODEL), lambda i: (i, 0)),
                  pl.BlockSpec((None, 1, D_MODEL), lambda i: (layer, 0, 0)),
                  pl.BlockSpec((None, D_MODEL, QKV_COLS), lambda i: (layer, 0, 0)),
                  tab_spec, tab_spec, tab_spec],
        out_specs=out_specs,
        out_shape=out_shape,
        scratch_shapes=[pltpu.VMEM((2, TM, LANES), F32)],
        compiler_params=_params(1),
        name="qkv_proj",
    )(x, g, w, cos, sin_lo, sin_hi)


def _head_masks(rows):
    lane = lax.broadcasted_iota(jnp.int32, (rows, GROUP_W), 1)
    return [(lane >= h * HEAD_DIM) & (lane < (h + 1) * HEAD_DIM) for h in range(HEADS_PER_GROUP)]


def _dilated_body(a0_ref, a1_ref, a2_ref, ya_ref, o_scr, lse_scr, qmask_scr, bias_scr, *s_slots, seq):
    hmask = _head_masks(BAND)
    row = lax.broadcasted_iota(jnp.int32, (BAND, 2 * BAND), 0)
    col = lax.broadcasted_iota(jnp.int32, (BAND, 2 * BAND), 1)
    band = ((col < BAND) & (col >= row)) | ((col >= BAND) & (col - BAND <= row))
    bias_scr[1] = jnp.where(band, 0.0, NEG)
    bias_scr[0] = jnp.where(band & (col >= BAND), 0.0, NEG)
    for h in range(HEADS_PER_GROUP):
        qmask_scr[h] = jnp.where(hmask[h], HEAD_DIM ** -0.5, 0.0).astype(BF16)

    a_refs = (a0_ref, a1_ref, a2_ref)
    units = [(g, r, n) for g, (_, dil) in enumerate(DIL_PAIRS)
             for r in range(dil) for n in range(seq // dil // BAND)]

    def keys_of(unit, t):
        g, r, n = unit
        cur = a_refs[g][t, 0, r, n * BAND:(n + 1) * BAND, :]
        if seq // DIL_PAIRS[g][1] // BAND == 1:
            return cur
        p = max(n - 1, 0)
        return jnp.concatenate([a_refs[g][t, 0, r, p * BAND:(p + 1) * BAND, :], cur], axis=0)

    def scores(unit, slot):
        g, r, n = unit
        q = a_refs[g][0, 0, r, n * BAND:(n + 1) * BAND, :]
        q4 = jnp.concatenate([q * qmask_scr[h] for h in range(HEADS_PER_GROUP)], axis=0)
        k = keys_of(unit, 1)
        nk = k.shape[0]
        bias = bias_scr[min(n, 1)] if nk == 2 * BAND else bias_scr[1, :, BAND:]
        s = _dot_nt(q4, k).reshape(HEADS_PER_GROUP, BAND, nk)
        slot[:, 0:nk] = (s + bias[None]).reshape(HEADS_PER_GROUP * BAND, nk)

    n_slot = len(s_slots)
    for i in range(min(DIL_LOOKAHEAD, len(units))):
        scores(units[i], s_slots[i % n_slot])
    for i, unit in enumerate(units):
        if i + DIL_LOOKAHEAD < len(units):
            scores(units[i + DIL_LOOKAHEAD], s_slots[(i + DIL_LOOKAHEAD) % n_slot])
        g, r, n = unit
        dil = DIL_PAIRS[g][1]
        v = keys_of(unit, 2)
        s = s_slots[i % n_slot][:, 0:v.shape[0]]
        m = jnp.max(s, axis=-1, keepdims=True)
        p = jnp.exp(s - m)
        l = jnp.sum(p, axis=-1, keepdims=True)
        o4 = _dot(p.astype(BF16), v) * (1.0 / l)
        lse4 = jnp.broadcast_to(m + jnp.log(l), (HEADS_PER_GROUP * BAND, GROUP_W))
        o = o4[0:BAND]
        lse = lse4[0:BAND]
        for h in range(1, HEADS_PER_GROUP):
            o = jnp.where(hmask[h], o4[h * BAND:(h + 1) * BAND], o)
            lse = jnp.where(hmask[h], lse4[h * BAND:(h + 1) * BAND], lse)
        start = n * BAND * dil + r
        dst = pl.ds(start, BAND) if dil == 1 else pl.ds(start, BAND, stride=dil)
        for slab in range(GROUP_W // LANES):
            o_scr[g, slab, dst, :] = o[:, slab * LANES:(slab + 1) * LANES]
            lse_scr[g, slab, dst, :] = lse[:, slab * LANES:(slab + 1) * LANES]

    chunk = 256
    for c in range(seq // chunk):
        rows = pl.ds(c * chunk, chunk)
        for slab in range(GROUP_W // LANES):
            lses = [lse_scr[g, slab, rows, :] for g in range(len(DIL_PAIRS))]
            top = jnp.maximum(jnp.maximum(lses[0], lses[1]), lses[2])
            es = [jnp.exp(l - top) for l in lses]
            num = sum(e * o_scr[g, slab, rows, :] for g, e in enumerate(es))
            ya_ref[rows, slab * LANES:(slab + 1) * LANES] = (num / (es[0] + es[1] + es[2])).astype(BF16)


def _dilated_call(a0, a1, a2, batch, seq):
    def a_spec(dil):
        return pl.BlockSpec((3, 1, dil, seq // dil, GROUP_W), lambda b: (0, b, 0, 0, 0))

    n_groups = len(DIL_PAIRS)
    return pl.pallas_call(
        functools.partial(_dilated_body, seq=seq),
        grid=(batch,),
        in_specs=[a_spec(dil) for _, dil in DIL_PAIRS],
        out_specs=pl.BlockSpec((seq, GROUP_W), lambda b: (b, 0)),
        out_shape=jax.ShapeDtypeStruct((batch * seq, GROUP_W), BF16),
        scratch_shapes=[pltpu.VMEM((n_groups, GROUP_W // LANES, seq, LANES), F32),
                        pltpu.VMEM((n_groups, GROUP_W // LANES, seq, LANES), F32),
                        pltpu.VMEM((HEADS_PER_GROUP, BAND, GROUP_W), BF16),
                        pltpu.VMEM((2, BAND, 2 * BAND), F32)]
                       + [pltpu.VMEM((HEADS_PER_GROUP * BAND, 2 * BAND), F32)] * (DIL_LOOKAHEAD + 1),
        compiler_params=_params(1),
        name="dilated_attn",
    )(a0, a1, a2)


def _fold_rows(tiles, op):
    parts = []
    for t in tiles:
        while t.shape[0] > 8:
            half = t.shape[0] // 2
            t = op(t[:half], t[half:])
        parts.append(t)
    while len(parts) > 1:
        parts = [op(parts[i], parts[i + 1]) if i + 1 < len(parts) else parts[i]
                 for i in range(0, len(parts), 2)]
    return parts[0]


def _moba_body(q_ref, k_ref, v_ref, km_ref, yb_ref, vt_scr, qmask_scr, *s_slots, seq):
    nblk = seq // MOBA_BLOCK
    hmask = _head_masks(MOBA_BLOCK)
    key = lax.broadcasted_iota(jnp.int32, (MOBA_BLOCK, MOBA_BLOCK), 0)
    qry = lax.broadcasted_iota(jnp.int32, (MOBA_BLOCK, MOBA_BLOCK), 1)
    causal = key <= qry
    km = km_ref[...]
    km_hi = km.astype(BF16)
    km_lo = (km - km_hi.astype(F32)).astype(BF16)
    blk_id = lax.broadcasted_iota(jnp.int32, (nblk, MOBA_BLOCK), 0)
    for h in range(HEADS_PER_GROUP):
        qmask_scr[h] = jnp.where(hmask[h], 1.0, 0.0).astype(BF16)
    heads = range(HEADS_PER_GROUP)
    for h in heads:
        vt_scr[h * VT_ROWS + HEAD_DIM:(h + 1) * VT_ROWS, :] = jnp.ones((VT_ROWS - HEAD_DIM, seq), BF16)
    for j in range(nblk):
        blk = slice(j * MOBA_BLOCK, (j + 1) * MOBA_BLOCK)
        vt = v_ref[blk, :].T
        for h in heads:
            vt_scr[h * VT_ROWS:h * VT_ROWS + HEAD_DIM, blk] = vt[h * HEAD_DIM:(h + 1) * HEAD_DIM]

    units = [(j, n, h) for j in range(nblk) for n in [j] + list(range(j)) for h in heads]
    qh, pen = {}, {}

    def scores(unit):
        j, n, h = unit
        if (j, h) not in qh:
            qh[j, h] = q_ref[j * MOBA_BLOCK:(j + 1) * MOBA_BLOCK, :] * qmask_scr[h]
        return _dot_nt(k_ref[n * MOBA_BLOCK:(n + 1) * MOBA_BLOCK, :], qh[j, h])

    def penalty(j, h):
        if (j, h) not in pen:
            gate = _dot_nt(km_hi, qh[j, h]) + _dot_nt(km_lo, qh[j, h])
            beaten = jnp.zeros((nblk, MOBA_BLOCK), F32)
            for m in range(j):
                gm = gate[m:m + 1, :]
                beats = (gm > gate) | ((gm == gate) & (blk_id > m))
                beaten = beaten + jnp.where(beats, 1.0, 0.0)
            pen[j, h] = jnp.where(beaten >= MOBA_TOPK, NEG, 0.0)
        return pen[j, h]

    nslot = len(s_slots)
    for i in range(min(MOBA_LOOKAHEAD, len(units))):
        s_slots[i % nslot][...] = scores(units[i])
    m_run, l_run, acc = {}, {}, {}
    for i, (j, n, h) in enumerate(units):
        if i + MOBA_LOOKAHEAD < len(units):
            s_slots[(i + MOBA_LOOKAHEAD) % nslot][...] = scores(units[i + MOBA_LOOKAHEAD])
        s = s_slots[i % nslot][...]
        if n == j and h == 0 and j + 1 < nblk and j + 1 > MOBA_TOPK:
            for hh in heads:
                if (j + 1, hh) not in qh:
                    qh[j + 1, hh] = q_ref[(j + 1) * MOBA_BLOCK:(j + 2) * MOBA_BLOCK, :] * qmask_scr[hh]
                penalty(j + 1, hh)
        if n == j:
            s = jnp.where(causal, s, NEG)
        elif j > MOBA_TOPK:
            s = s + penalty(j, h)[n:n + 1, :]
        m_blk = jnp.max(_fold_rows([s], jnp.maximum), axis=0, keepdims=True)
        m_new = m_blk if n == j else jnp.maximum(m_run[h], m_blk)
        p = jnp.exp2(s - m_new).astype(BF16)
        vt = vt_scr[h * VT_ROWS:(h + 1) * VT_ROWS, n * MOBA_BLOCK:(n + 1) * MOBA_BLOCK]
        pv = _dot(vt, p)
        acc[h] = pv if n == j else jnp.exp2(m_run[h] - m_new) * acc[h] + pv
        m_run[h] = m_new
        if n == max(j - 1, 0) and h == HEADS_PER_GROUP - 1:
            out_t = jnp.concatenate(
                [acc[hh][:HEAD_DIM] * (1.0 / acc[hh][HEAD_DIM:HEAD_DIM + 1]) for hh in heads], axis=0)
            yb_ref[j * MOBA_BLOCK:(j + 1) * MOBA_BLOCK, :] = out_t.T.astype(BF16)


def _moba_call(bqkv, kmean, batch, seq):
    def qkv_spec(t):
        return pl.BlockSpec((None, seq, GROUP_W), lambda b, hg: (t, b, hg))

    nblk = seq // MOBA_BLOCK
    return pl.pallas_call(
        functools.partial(_moba_body, seq=seq),
        grid=(batch, WIDTH_B // GROUP_W),
        in_specs=[qkv_spec(0), qkv_spec(1), qkv_spec(2),
                  pl.BlockSpec((None, nblk, GROUP_W), lambda b, hg: (b, 0, hg))],
        out_specs=pl.BlockSpec((seq, GROUP_W), lambda b, hg: (b, hg)),
        out_shape=jax.ShapeDtypeStruct((batch * seq, WIDTH_B), BF16),
        scratch_shapes=[pltpu.VMEM((HEADS_PER_GROUP * VT_ROWS, seq), BF16),
                        pltpu.VMEM((HEADS_PER_GROUP, MOBA_BLOCK, GROUP_W), BF16)]
                       + [pltpu.VMEM((MOBA_BLOCK, MOBA_BLOCK), F32)] * (MOBA_LOOKAHEAD + 1),
        compiler_params=_params(2),
        name="moba_attn",
    )(bqkv, bqkv, bqkv, kmean)


def _mix_body(x_ref, ya_ref, yb_ref, gpre_ref, wg_ref, bg_ref, wpa_ref, wpb_ref, wo_ref, gpost_ref,
              out_ref):
    x = x_ref[...]
    h = _rms(x, gpre_ref[...]).astype(BF16)
    bias = bg_ref[...]
    gate_a = jax.nn.sigmoid(_dot(h, wg_ref[:, QKV_COLS:QKV_COLS + D_MODEL]) + bias[:, :D_MODEL])
    merged = gate_a * _dot(ya_ref[...], wpa_ref[...])
    gate_b = jax.nn.sigmoid(_dot(h, wg_ref[:, QKV_COLS + D_MODEL:]) + bias[:, D_MODEL:])
    merged = merged + gate_b * _dot(yb_ref[...], wpb_ref[...])
    out_ref[...] = x + _rms(_dot(merged.astype(BF16), wo_ref[...]), gpost_ref[...])


def _mix_call(layer, x, ya, yb, gpre, wg, bg, wpa, wpb, wo, gpost):
    tokens = x.shape[0]

    def full(shape):
        return pl.BlockSpec((None,) + shape, lambda i: (layer,) + (0,) * len(shape))

    def rows(width):
        return pl.BlockSpec((TM, width), lambda i: (i, 0))

    return pl.pallas_call(
        _mix_body,
        grid=(tokens // TM,),
        in_specs=[rows(D_MODEL), rows(GROUP_W), rows(WIDTH_B), full((1, D_MODEL)),
                  full((D_MODEL, D_IN)), full((1, 2 * D_MODEL)),
                  full((GROUP_W, D_MODEL)), full((WIDTH_B, D_MODEL)), full((D_MODEL, D_MODEL)),
                  full((1, D_MODEL))],
        out_specs=rows(D_MODEL),
        out_shape=jax.ShapeDtypeStruct((tokens, D_MODEL), F32),
        compiler_params=_params(1),
        name="post_mixer",
    )(x, ya, yb, gpre, wg, bg, wpa, wpb, wo, gpost)


def _ffn_body(x_ref, halo_ref, gpre_ref, wug_ref, wuv_ref, cwg_ref, cwv_ref, cbg_ref, cbv_ref,
              wd_ref, gpost_ref, out_ref, h_scr, acc_scr, act_scr, slab_scr, *, tiles_per_seq):
    i = pl.program_id(0)
    c = pl.program_id(1)
    n_slab = D_MODEL // LANES

    @pl.when(c == 0)
    def _():
        g = gpre_ref[...]
        keep = jnp.where(i % tiles_per_seq == 0, 0.0, 1.0)
        h_scr[0:HALO, :] = (_rms(halo_ref[...], g) * keep).astype(BF16)
        h = _rms(x_ref[...], g)
        for run in range(SUBLANES):
            for sl in range(n_slab):
                slab_scr[sl, pl.ds(run, FFN_RUN, stride=SUBLANES), :] = h[run * FFN_RUN:(run + 1) * FFN_RUN,
                                                                          sl * LANES:(sl + 1) * LANES]
        h_scr[HALO:, :] = jnp.concatenate([slab_scr[sl, 0:TM_FFN, :] for sl in range(n_slab)], axis=-1).astype(BF16)
        acc_scr[...] = jnp.zeros_like(acc_scr)

    h = h_scr[...]
    n_sub = FF_CHUNK // FF_SUB

    def up(s):
        cols = slice(s * FF_SUB, (s + 1) * FF_SUB)
        for t, w_ref in enumerate((wug_ref, wuv_ref)):
            u = _dot(h, w_ref[:, cols])
            for half in range(FF_SUB // LANES):
                slab_scr[(s % 2) * 4 + t * 2 + half] = u[:, half * LANES:(half + 1) * LANES]

    def parked(s, t):
        base = (s % 2) * 4 + t * 2
        return jnp.concatenate([slab_scr[base + half] for half in range(FF_SUB // LANES)], axis=-1)

    first_sublane = lax.broadcasted_iota(jnp.int32, (SUBLANES, FF_SUB), 0) == 0

    def conv(u, cw_ref, cb_ref, cols):
        cw = cw_ref[:, cols]

        def before_first(back):
            last = u[HALO + TM_FFN - back * SUBLANES:HALO + TM_FFN - (back - 1) * SUBLANES]
            return jnp.where(first_sublane, u[HALO - back:HALO - back + 1], pltpu.roll(last, 1, 0))

        ext = jnp.concatenate([before_first(2), before_first(1), u[HALO:]], axis=0)
        return (ext[0:TM_FFN] * cw[0:1] + ext[SUBLANES:SUBLANES + TM_FFN] * cw[1:2]
                + ext[2 * SUBLANES:] * cw[2:3] + cb_ref[:, cols])

    def down(half):
        rows = slice(half * (FF_CHUNK // 2), (half + 1) * (FF_CHUNK // 2))
        acc_scr[...] += _dot(act_scr[:, rows], wd_ref[rows, :])

    up(0)
    for s in range(n_sub):
        if s + 1 < n_sub:
            up(s + 1)
        if s == n_sub // 2 + 1:
            down(0)
        cols = slice(s * FF_SUB, (s + 1) * FF_SUB)
        gate = conv(parked(s, 0), cwg_ref, cbg_ref, cols)
        val = conv(parked(s, 1), cwv_ref, cbv_ref, cols)
        act_scr[:, cols] = (jax.nn.gelu(gate, approximate=True) * val).astype(BF16)
    down(1)

    @pl.when(c == pl.num_programs(1) - 1)
    def _():
        y = _rms(acc_scr[...], gpost_ref[...])
        for sl in range(n_slab):
            slab_scr[sl, 0:TM_FFN, :] = y[:, sl * LANES:(sl + 1) * LANES]
        for run in range(SUBLANES):
            rows = slice(run * FFN_RUN, (run + 1) * FFN_RUN)
            y_run = jnp.concatenate([slab_scr[sl, pl.ds(run, FFN_RUN, stride=SUBLANES), :]
                                     for sl in range(n_slab)], axis=-1)
            out_ref[rows, :] = x_ref[rows, :] + y_run


def _ffn_call(layer, x, gpre, wu, cw, cb, wd, gpost, seq):
    tokens = x.shape[0]
    n_chunks = D_FF // FF_CHUNK
    halo_per_tile = TM_FFN // HALO
    vec = pl.BlockSpec((None, 1, D_MODEL), lambda i, c: (layer, 0, 0))
    return pl.pallas_call(
        functools.partial(_ffn_body, tiles_per_seq=seq // TM_FFN),
        grid=(tokens // TM_FFN, n_chunks),
        in_specs=[pl.BlockSpec((TM_FFN, D_MODEL), lambda i, c: (i, 0)),
                  pl.BlockSpec((HALO, D_MODEL), lambda i, c: (jnp.maximum(i * halo_per_tile - 1, 0), 0)),
                  vec,
                  pl.BlockSpec((None, D_MODEL, FF_CHUNK), lambda i, c: (layer, 0, c)),
                  pl.BlockSpec((None, D_MODEL, FF_CHUNK), lambda i, c: (layer, 0, n_chunks + c)),
                  pl.BlockSpec((None, 3, FF_CHUNK), lambda i, c: (layer, 0, c)),
                  pl.BlockSpec((None, 3, FF_CHUNK), lambda i, c: (layer, 0, n_chunks + c)),
                  pl.BlockSpec((None, 1, FF_CHUNK), lambda i, c: (layer, 0, c)),
                  pl.BlockSpec((None, 1, FF_CHUNK), lambda i, c: (layer, 0, n_chunks + c)),
                  pl.BlockSpec((None, FF_CHUNK, D_MODEL), lambda i, c: (layer, c, 0)),
                  vec],
        out_specs=pl.BlockSpec((TM_FFN, D_MODEL), lambda i, c: (i, 0)),
        out_shape=jax.ShapeDtypeStruct((tokens, D_MODEL), F32),
        scratch_shapes=[pltpu.VMEM((HALO + TM_FFN, D_MODEL), BF16), pltpu.VMEM((TM_FFN, D_MODEL), F32),
                        pltpu.VMEM((TM_FFN, FF_CHUNK), BF16),
                        pltpu.VMEM((D_MODEL // LANES, HALO + TM_FFN, LANES), F32)],
        compiler_params=_params(2),
        name="conv_ffn",
    )(x, x, gpre, wu, wu, cw, cw, cb, cb, wd, gpost)


def _rope_tables(positions):
    inv_freq = ROPE_THETA ** (-jnp.arange(0, ROT_DIM, 2, dtype=F32) / ROT_DIM)
    lane = jnp.arange(LANES, dtype=jnp.int32) % HEAD_DIM
    ang = positions.astype(F32).reshape(-1, 1) * inv_freq[lane % ROT_HALF][None, :]
    cos = jnp.where(lane < ROT_DIM, jnp.cos(ang), 1.0)
    sin = jnp.sin(ang)
    return (cos, jnp.where(lane < ROT_HALF, -sin, 0.0),
            jnp.where((lane >= ROT_HALF) & (lane < ROT_DIM), sin, 0.0))


def kernel(x, positions, norm_pre_mix, w_in, b_gate, w_proj_a, w_proj_b, w_out, norm_post_mix,
           norm_pre_ffn, w_up, conv_w, conv_b, w_down, norm_post_ffn):
    batch, seq, _ = x.shape
    depth = w_in.shape[0]
    assert seq % TM == 0 and seq % TM_FFN == 0 and TM % MOBA_BLOCK == 0 and seq // DIL_PAIRS[-1][1] == BAND
    cos, sin_lo, sin_hi = _rope_tables(positions)
    w_i = w_in.astype(BF16)
    w_pa, w_pb, w_o = w_proj_a.astype(BF16), w_proj_b.astype(BF16), w_out.astype(BF16)
    w_u, w_d = w_up.astype(BF16), w_down.astype(BF16)
    rows = lambda p: p.reshape(depth, 1, -1)
    g_pre_mix, g_post_mix = rows(norm_pre_mix), rows(norm_post_mix)
    g_pre_ffn, g_post_ffn = rows(norm_pre_ffn), rows(norm_post_ffn)
    bias_gate, bias_conv = rows(b_gate), rows(conv_b)

    xt = x.reshape(batch * seq, D_MODEL)
    for l in range(depth):
        a0, a1, a2, bqkv, kmean = _qkv_call(l, xt, g_pre_mix, w_i, cos, sin_lo, sin_hi, batch, seq)
        ya = _dilated_call(a0, a1, a2, batch, seq)
        yb = _moba_call(bqkv, kmean.reshape(batch, seq // MOBA_BLOCK, WIDTH_B), batch, seq)
        xt = _mix_call(l, xt, ya, yb, g_pre_mix, w_i, bias_gate, w_pa, w_pb, w_o, g_post_mix)
        xt = _ffn_call(l, xt, g_pre_ffn, w_u, conv_w, bias_conv, w_d, g_post_ffn, seq)
    return xt.reshape(batch, seq, D_MODEL)
```

```python
import functools

import jax
import jax.numpy as jnp
from jax import lax
from jax.experimental import pallas as pl
from jax.experimental.pallas import tpu as pltpu

F32 = jnp.float32
BF16 = jnp.bfloat16

D_MODEL = 1024
HEAD_DIM = 64
DIL_PAIRS = ((128, 1), (512, 4), (2048, 16))
BAND = 128
HEADS_PER_GROUP = 4
GROUP_W = HEADS_PER_GROUP * HEAD_DIM
WIDTH_A = 3 * GROUP_W
WIDTH_B = 512
MOBA_BLOCK = 256
MOBA_TOPK = 3
ROPE_THETA = 500000.0
ROT_DIM = HEAD_DIM // 4
ROT_HALF = ROT_DIM // 2
D_FF = 4 * D_MODEL
RMS_EPS = 1e-6
QKV_COLS = 3 * WIDTH_A + 3 * WIDTH_B
D_IN = QKV_COLS + 2 * D_MODEL
LANES = 128
NEG = -1e30

TM = 1024
TM_FFN = 1024
SUBLANES = 8
FFN_RUN = TM_FFN // SUBLANES
FF_CHUNK = 1024
FF_SUB = 512
MOBA_Q_SCALE = HEAD_DIM ** -0.5 * 1.4426950408889634
MOBA_LOOKAHEAD = 7
VT_ROWS = HEAD_DIM + 16
DIL_LOOKAHEAD = 2
HALO = 16
VMEM_LIMIT = 56 * 1024 * 1024


def _rms(x, g):
    return x * lax.rsqrt(jnp.mean(x * x, axis=-1, keepdims=True) + RMS_EPS) * g


def _dot(a, b):
    return jnp.dot(a, b, preferred_element_type=F32)


def _dot_nt(a, b):
    return lax.dot_general(a, b, (((1,), (1,)), ((), ())), preferred_element_type=F32)


def _params(n_grid):
    return pltpu.CompilerParams(dimension_semantics=("arbitrary",) * n_grid,
                                vmem_limit_bytes=VMEM_LIMIT)


def _qkv_body(x_ref, g_ref, w_ref, c_ref, s1_ref, s2_ref,
              a0_ref, a1_ref, a2_ref, b_ref, km_ref, res_ref):
    h = _rms(x_ref[...], g_ref[...]).astype(BF16)
    cos = jnp.concatenate([c_ref[...]] * 2, axis=-1)
    sin_lo = jnp.concatenate([s1_ref[...]] * 2, axis=-1)
    sin_hi = jnp.concatenate([s2_ref[...]] * 2, axis=-1)

    def proj(c0, rope):
        t = _dot(h, w_ref[:, c0:c0 + GROUP_W])
        if rope:
            t = (t * cos + pltpu.roll(t, GROUP_W - ROT_HALF, 1) * sin_lo
                 + pltpu.roll(t, ROT_HALF, 1) * sin_hi)
        return t

    a_refs = (a0_ref, a1_ref, a2_ref)
    for g, (_, dil) in enumerate(DIL_PAIRS):
        for t in range(3):
            res = proj(t * WIDTH_A + g * GROUP_W, rope=t < 2)
            if dil == 1:
                a_refs[g][t, 0, 0] = res.astype(BF16)
            else:
                res_ref[0] = res[:, :LANES]
                res_ref[1] = res[:, LANES:]
                n = TM // dil
                for r in range(dil):
                    piece = jnp.concatenate(
                        [res_ref[0, pl.ds(r, n, stride=dil), :],
                         res_ref[1, pl.ds(r, n, stride=dil), :]], axis=-1)
                    a_refs[g][t, 0, r] = piece.astype(BF16)
    for t in range(3):
        for cc in range(WIDTH_B // GROUP_W):
            lo = cc * GROUP_W
            res = proj(3 * WIDTH_A + t * WIDTH_B + lo, rope=t < 2)
            if t == 0:
                res = res * MOBA_Q_SCALE
            b_ref[t, :, lo:lo + GROUP_W] = res.astype(BF16)
            if t == 1:
                for blk in range(TM // MOBA_BLOCK):
                    km_ref[0, blk:blk + 1, lo:lo + GROUP_W] = jnp.mean(
                        res[blk * MOBA_BLOCK:(blk + 1) * MOBA_BLOCK], axis=0, keepdims=True)


def _qkv_call(layer, x, g, w, cos, sin_lo, sin_hi, batch, seq):
    tokens = x.shape[0]
    tiles_per_seq = seq // TM
    tab_spec = pl.BlockSpec((TM, LANES), lambda i: (i, 0))

    def a_spec(dil):
        return pl.BlockSpec((3, 1, dil, TM // dil, GROUP_W),
                            lambda i: (0, i // tiles_per_seq, 0, i % tiles_per_seq, 0))

    out_shape = [jax.ShapeDtypeStruct((3, batch, dil, seq // dil, GROUP_W), BF16) for _, dil in DIL_PAIRS]
    out_shape += [jax.ShapeDtypeStruct((3, tokens, WIDTH_B), BF16),
                  jax.ShapeDtypeStruct((tokens // TM, TM // MOBA_BLOCK, WIDTH_B), F32)]
    out_specs = [a_spec(dil) for _, dil in DIL_PAIRS]
    out_specs += [pl.BlockSpec((3, TM, WIDTH_B), lambda i: (0, i, 0)),
                  pl.BlockSpec((1, TM // MOBA_BLOCK, WIDTH_B), lambda i: (i, 0, 0))]
    return pl.pallas_call(
        _qkv_body,
        grid=(tokens // TM,),
        in_specs=[pl.BlockSpec((TM, D_MODEL), lambda i: (i, 0)),
                  pl.BlockSpec((None, 1, D_MODEL), lambda i: (layer, 0, 0)),
                  pl.BlockSpec((None, D_MODEL, QKV_COLS), lambda i: (layer, 0, 0)),
                  tab_spec, tab_spec, tab_spec],
        out_specs=out_specs,
        out_shape=out_shape,
        scratch_shapes=[pltpu.VMEM((2, TM, LANES), F32)],
        compiler_params=_params(1),
        name="qkv_proj",
    )(x, g, w, cos, sin_lo, sin_hi)


def _head_masks(rows):
    lane = lax.broadcasted_iota(jnp.int32, (rows, GROUP_W), 1)
    return [(lane >= h * HEAD_DIM) & (lane < (h + 1) * HEAD_DIM) for h in range(HEADS_PER_GROUP)]


def _dilated_body(a0_ref, a1_ref, a2_ref, ya_ref, o_scr, lse_scr, qmask_scr, bias_scr, *s_slots, seq):
    hmask = _head_masks(BAND)
    row = lax.broadcasted_iota(jnp.int32, (BAND, 2 * BAND), 0)
    col = lax.broadcasted_iota(jnp.int32, (BAND, 2 * BAND), 1)
    band = ((col < BAND) & (col >= row)) | ((col >= BAND) & (col - BAND <= row))
    bias_scr[1] = jnp.where(band, 0.0, NEG)
    bias_scr[0] = jnp.where(band & (col >= BAND), 0.0, NEG)
    for h in range(HEADS_PER_GROUP):
        qmask_scr[h] = jnp.where(hmask[h], HEAD_DIM ** -0.5, 0.0).astype(BF16)

    a_refs = (a0_ref, a1_ref, a2_ref)
    units = [(g, r, n) for g, (_, dil) in enumerate(DIL_PAIRS)
             for r in range(dil) for n in range(seq // dil // BAND)]

    def keys_of(unit, t):
        g, r, n = unit
        cur = a_refs[g][t, 0, r, n * BAND:(n + 1) * BAND, :]
        if seq // DIL_PAIRS[g][1] // BAND == 1:
            return cur
        p = max(n - 1, 0)
        return jnp.concatenate([a_refs[g][t, 0, r, p * BAND:(p + 1) * BAND, :], cur], axis=0)

    def scores(unit, slot):
        g, r, n = unit
        q = a_refs[g][0, 0, r, n * BAND:(n + 1) * BAND, :]
        q4 = jnp.concatenate([q * qmask_scr[h] for h in range(HEADS_PER_GROUP)], axis=0)
        k = keys_of(unit, 1)
        nk = k.shape[0]
        bias = bias_scr[min(n, 1)] if nk == 2 * BAND else bias_scr[1, :, BAND:]
        s = _dot_nt(q4, k).reshape(HEADS_PER_GROUP, BAND, nk)
        slot[:, 0:nk] = (s + bias[None]).reshape(HEADS_PER_GROUP * BAND, nk)

    n_slot = len(s_slots)
    for i in range(min(DIL_LOOKAHEAD, len(units))):
        scores(units[i], s_slots[i % n_slot])
    for i, unit in enumerate(units):
        if i + DIL_LOOKAHEAD < len(units):
            scores(units[i + DIL_LOOKAHEAD], s_slots[(i + DIL_LOOKAHEAD) % n_slot])
        g, r, n = unit
        dil = DIL_PAIRS[g][1]
        v = keys_of(unit, 2)
        s = s_slots[i % n_slot][:, 0:v.shape[0]]
        m = jnp.max(s, axis=-1, keepdims=True)
        p = jnp.exp(s - m)
        l = jnp.sum(p, axis=-1, keepdims=True)
        o4 = _dot(p.astype(BF16), v) * (1.0 / l)
        lse4 = jnp.broadcast_to(m + jnp.log(l), (HEADS_PER_GROUP * BAND, GROUP_W))
        o = o4[0:BAND]
        lse = lse4[0:BAND]
        for h in range(1, HEADS_PER_GROUP):
            o = jnp.where(hmask[h], o4[h * BAND:(h + 1) * BAND], o)
            lse = jnp.where(hmask[h], lse4[h * BAND:(h + 1) * BAND], lse)
        start = n * BAND * dil + r
        dst = pl.ds(start, BAND) if dil == 1 else pl.ds(start, BAND, stride=dil)
        for slab in range(GROUP_W // LANES):
            o_scr[g, slab, dst, :] = o[:, slab * LANES:(slab + 1) * LANES]
            lse_scr[g, slab, dst, :] = lse[:, slab * LANES:(slab + 1) * LANES]

    chunk = 256
    for c in range(seq // chunk):
        rows = pl.ds(c * chunk, chunk)
        for slab in range(GROUP_W // LANES):
            lses = [lse_scr[g, slab, rows, :] for g in range(len(DIL_PAIRS))]
            top = jnp.maximum(jnp.maximum(lses[0], lses[1]), lses[2])
            es = [jnp.exp(l - top) for l in lses]
            num = sum(e * o_scr[g, slab, rows, :] for g, e in enumerate(es))
            ya_ref[rows, slab * LANES:(slab + 1) * LANES] = (num / (es[0] + es[1] + es[2])).astype(BF16)


def _dilated_call(a0, a1, a2, batch, seq):
    def a_spec(dil):
        return pl.BlockSpec((3, 1, dil, seq // dil, GROUP_W), lambda b: (0, b, 0, 0, 0))

    n_groups = len(DIL_PAIRS)
    return pl.pallas_call(
        functools.partial(_dilated_body, seq=seq),
        grid=(batch,),
        in_specs=[a_spec(dil) for _, dil in DIL_PAIRS],
        out_specs=pl.BlockSpec((seq, GROUP_W), lambda b: (b, 0)),
        out_shape=jax.ShapeDtypeStruct((batch * seq, GROUP_W), BF16),
        scratch_shapes=[pltpu.VMEM((n_groups, GROUP_W // LANES, seq, LANES), F32),
                        pltpu.VMEM((n_groups, GROUP_W // LANES, seq, LANES), F32),
                        pltpu.VMEM((HEADS_PER_GROUP, BAND, GROUP_W), BF16),
                        pltpu.VMEM((2, BAND, 2 * BAND), F32)]
                       + [pltpu.VMEM((HEADS_PER_GROUP * BAND, 2 * BAND), F32)] * (DIL_LOOKAHEAD + 1),
        compiler_params=_params(1),
        name="dilated_attn",
    )(a0, a1, a2)


def _fold_rows(tiles, op):
    parts = []
    for t in tiles:
        while t.shape[0] > 8:
            half = t.shape[0] // 2
            t = op(t[:half], t[half:])
        parts.append(t)
    while len(parts) > 1:
        parts = [op(parts[i], parts[i + 1]) if i + 1 < len(parts) else parts[i]
                 for i in range(0, len(parts), 2)]
    return parts[0]


def _moba_body(q_ref, k_ref, v_ref, km_ref, yb_ref, vt_scr, qmask_scr, *s_slots, seq):
    nblk = seq // MOBA_BLOCK
    hmask = _head_masks(MOBA_BLOCK)
    key = lax.broadcasted_iota(jnp.int32, (MOBA_BLOCK, MOBA_BLOCK), 0)
    qry = lax.broadcasted_iota(jnp.int32, (MOBA_BLOCK, MOBA_BLOCK), 1)
    causal = key <= qry
    km = km_ref[...]
    km_hi = km.astype(BF16)
    km_lo = (km - km_hi.astype(F32)).astype(BF16)
    blk_id = lax.broadcasted_iota(jnp.int32, (nblk, MOBA_BLOCK), 0)
    for h in range(HEADS_PER_GROUP):
        qmask_scr[h] = jnp.where(hmask[h], 1.0, 0.0).astype(BF16)
    heads = range(HEADS_PER_GROUP)
    for h in heads:
        vt_scr[h * VT_ROWS + HEAD_DIM:(h + 1) * VT_ROWS, :] = jnp.ones((VT_ROWS - HEAD_DIM, seq), BF16)
    for j in range(nblk):
        blk = slice(j * MOBA_BLOCK, (j + 1) * MOBA_BLOCK)
        vt = v_ref[blk, :].T
        for h in heads:
            vt_scr[h * VT_ROWS:h * VT_ROWS + HEAD_DIM, blk] = vt[h * HEAD_DIM:(h + 1) * HEAD_DIM]

    units = [(j, n, h) for j in range(nblk) for n in [j] + list(range(j)) for h in heads]
    qh, pen = {}, {}

    def scores(unit):
        j, n, h = unit
        if (j, h) not in qh:
            qh[j, h] = q_ref[j * MOBA_BLOCK:(j + 1) * MOBA_BLOCK, :] * qmask_scr[h]
        return _dot_nt(k_ref[n * MOBA_BLOCK:(n + 1) * MOBA_BLOCK, :], qh[j, h])

    def penalty(j, h):
        if (j, h) not in pen:
            gate = _dot_nt(km_hi, qh[j, h]) + _dot_nt(km_lo, qh[j, h])
            beaten = jnp.zeros((nblk, MOBA_BLOCK), F32)
            for m in range(j):
                gm = gate[m:m + 1, :]
                beats = (gm > gate) | ((gm == gate) & (blk_id > m))
                beaten = beaten + jnp.where(beats, 1.0, 0.0)
            pen[j, h] = jnp.where(beaten >= MOBA_TOPK, NEG, 0.0)
        return pen[j, h]

    nslot = len(s_slots)
    for i in range(min(MOBA_LOOKAHEAD, len(units))):
        s_slots[i % nslot][...] = scores(units[i])
    m_run, l_run, acc = {}, {}, {}
    for i, (j, n, h) in enumerate(units):
        if i + MOBA_LOOKAHEAD < len(units):
            s_slots[(i + MOBA_LOOKAHEAD) % nslot][...] = scores(units[i + MOBA_LOOKAHEAD])
        s = s_slots[i % nslot][...]
        if n == j and h == 0 and j + 1 < nblk and j + 1 > MOBA_TOPK:
            for hh in heads:
                if (j + 1, hh) not in qh:
                    qh[j + 1, hh] = q_ref[(j + 1) * MOBA_BLOCK:(j + 2) * MOBA_BLOCK, :] * qmask_scr[hh]
                penalty(j + 1, hh)
        if n == j:
            s = jnp.where(causal, s, NEG)
        elif j > MOBA_TOPK:
            s = s + penalty(j, h)[n:n + 1, :]
        m_blk = jnp.max(_fold_rows([s], jnp.maximum), axis=0, keepdims=True)
        m_new = m_blk if n == j else jnp.maximum(m_run[h], m_blk)
        p = jnp.exp2(s - m_new).astype(BF16)
        vt = vt_scr[h * VT_ROWS:(h + 1) * VT_ROWS, n * MOBA_BLOCK:(n + 1) * MOBA_BLOCK]
        pv = _dot(vt, p)
        acc[h] = pv if n == j else jnp.exp2(m_run[h] - m_new) * acc[h] + pv
        m_run[h] = m_new
        if n == max(j - 1, 0) and h == HEADS_PER_GROUP - 1:
            out_t = jnp.concatenate(
                [acc[hh][:HEAD_DIM] * (1.0 / acc[hh][HEAD_DIM:HEAD_DIM + 1]) for hh in heads], axis=0)
            yb_ref[j * MOBA_BLOCK:(j + 1) * MOBA_BLOCK, :] = out_t.T.astype(BF16)


def _moba_call(bqkv, kmean, batch, seq):
    def qkv_spec(t):
        return pl.BlockSpec((None, seq, GROUP_W), lambda b, hg: (t, b, hg))

    nblk = seq // MOBA_BLOCK
    return pl.pallas_call(
        functools.partial(_moba_body, seq=seq),
        grid=(batch, WIDTH_B // GROUP_W),
        in_specs=[qkv_spec(0), qkv_spec(1), qkv_spec(2),
                  pl.BlockSpec((None, nblk, GROUP_W), lambda b, hg: (b, 0, hg))],
        out_specs=pl.BlockSpec((seq, GROUP_W), lambda b, hg: (b, hg)),
        out_shape=jax.ShapeDtypeStruct((batch * seq, WIDTH_B), BF16),
        scratch_shapes=[pltpu.VMEM((HEADS_PER_GROUP * VT_ROWS, seq), BF16),
                        pltpu.VMEM((HEADS_PER_GROUP, MOBA_BLOCK, GROUP_W), BF16)]
                       + [pltpu.VMEM((MOBA_BLOCK, MOBA_BLOCK), F32)] * (MOBA_LOOKAHEAD + 1),
        compiler_params=_params(2),
        name="moba_attn",
    )(bqkv, bqkv, bqkv, kmean)


def _mix_body(x_ref, ya_ref, yb_ref, gpre_ref, wg_ref, bg_ref, wpa_ref, wpb_ref, wo_ref, gpost_ref,
              out_ref):
    x = x_ref[...]
    h = _rms(x, gpre_ref[...]).astype(BF16)
    bias = bg_ref[...]
    gate_a = jax.nn.sigmoid(_dot(h, wg_ref[:, QKV_COLS:QKV_COLS + D_MODEL]) + bias[:, :D_MODEL])
    merged = gate_a * _dot(ya_ref[...], wpa_ref[...])
    gate_b = jax.nn.sigmoid(_dot(h, wg_ref[:, QKV_COLS + D_MODEL:]) + bias[:, D_MODEL:])
    merged = merged + gate_b * _dot(yb_ref[...], wpb_ref[...])
    out_ref[...] = x + _rms(_dot(merged.astype(BF16), wo_ref[...]), gpost_ref[...])


def _mix_call(layer, x, ya, yb, gpre, wg, bg, wpa, wpb, wo, gpost):
    tokens = x.shape[0]

    def full(shape):
        return pl.BlockSpec((None,) + shape, lambda i: (layer,) + (0,) * len(shape))

    def rows(width):
        return pl.BlockSpec((TM, width), lambda i: (i, 0))

    return pl.pallas_call(
        _mix_body,
        grid=(tokens // TM,),
        in_specs=[rows(D_MODEL), rows(GROUP_W), rows(WIDTH_B), full((1, D_MODEL)),
                  full((D_MODEL, D_IN)), full((1, 2 * D_MODEL)),
                  full((GROUP_W, D_MODEL)), full((WIDTH_B, D_MODEL)), full((D_MODEL, D_MODEL)),
                  full((1, D_MODEL))],
        out_specs=rows(D_MODEL),
        out_shape=jax.ShapeDtypeStruct((tokens, D_MODEL), F32),
        compiler_params=_params(1),
        name="post_mixer",
    )(x, ya, yb, gpre, wg, bg, wpa, wpb, wo, gpost)


def _ffn_body(x_ref, halo_ref, gpre_ref, wug_ref, wuv_ref, cwg_ref, cwv_ref, cbg_ref, cbv_ref,
              wd_ref, gpost_ref, out_ref, h_scr, acc_scr, act_scr, slab_scr, *, tiles_per_seq):
    i = pl.program_id(0)
    c = pl.program_id(1)
    n_slab = D_MODEL // LANES

    @pl.when(c == 0)
    def _():
        g = gpre_ref[...]
        keep = jnp.where(i % tiles_per_seq == 0, 0.0, 1.0)
        h_scr[0:HALO, :] = (_rms(halo_ref[...], g) * keep).astype(BF16)
        h = _rms(x_ref[...], g)
        for run in range(SUBLANES):
            for sl in range(n_slab):
                slab_scr[sl, pl.ds(run, FFN_RUN, stride=SUBLANES), :] = h[run * FFN_RUN:(run + 1) * FFN_RUN,
                                                                          sl * LANES:(sl + 1) * LANES]
        h_scr[HALO:, :] = jnp.concatenate([slab_scr[sl, 0:TM_FFN, :] for sl in range(n_slab)], axis=-1).astype(BF16)
        acc_scr[...] = jnp.zeros_like(acc_scr)

    h = h_scr[...]
    n_sub = FF_CHUNK // FF_SUB

    def up(s):
        cols = slice(s * FF_SUB, (s + 1) * FF_SUB)
        for t, w_ref in enumerate((wug_ref, wuv_ref)):
            u = _dot(h, w_ref[:, cols])
            for half in range(FF_SUB // LANES):
                slab_scr[((s % 2) * 2 + t) * (FF_SUB // LANES) + half] = u[:, half * LANES:(half + 1) * LANES]

    def parked(s, t):
        base = ((s % 2) * 2 + t) * (FF_SUB // LANES)
        return jnp.concatenate([slab_scr[base + half] for half in range(FF_SUB // LANES)], axis=-1)

    first_sublane = lax.broadcasted_iota(jnp.int32, (SUBLANES, FF_SUB), 0) == 0

    def conv(u, cw_ref, cb_ref, cols):
        cw = cw_ref[:, cols]

        def before_first(back):
            last = u[HALO + TM_FFN - back * SUBLANES:HALO + TM_FFN - (back - 1) * SUBLANES]
            return jnp.where(first_sublane, u[HALO - back:HALO - back + 1], pltpu.roll(last, 1, 0))

        ext = jnp.concatenate([before_first(2), before_first(1), u[HALO:]], axis=0)
        return (ext[0:TM_FFN] * cw[0:1] + ext[SUBLANES:SUBLANES + TM_FFN] * cw[1:2]
                + ext[2 * SUBLANES:] * cw[2:3] + cb_ref[:, cols])

    def down(half):
        rows = slice(half * (FF_CHUNK // 2), (half + 1) * (FF_CHUNK // 2))
        acc_scr[...] += _dot(act_scr[:, rows], wd_ref[rows, :])

    up(0)
    for s in range(n_sub):
        if s + 1 < n_sub:
            up(s + 1)
        if s == min(n_sub // 2 + 1, n_sub - 1):
            down(0)
        cols = slice(s * FF_SUB, (s + 1) * FF_SUB)
        gate = conv(parked(s, 0), cwg_ref, cbg_ref, cols)
        val = conv(parked(s, 1), cwv_ref, cbv_ref, cols)
        act_scr[:, cols] = (jax.nn.gelu(gate, approximate=True) * val).astype(BF16)
    down(1)

    @pl.when(c == pl.num_programs(1) - 1)
    def _():
        y = _rms(acc_scr[...], gpost_ref[...])
        for sl in range(n_slab):
            slab_scr[sl, 0:TM_FFN, :] = y[:, sl * LANES:(sl + 1) * LANES]
        for run in range(SUBLANES):
            rows = slice(run * FFN_RUN, (run + 1) * FFN_RUN)
            y_run = jnp.concatenate([slab_scr[sl, pl.ds(run, FFN_RUN, stride=SUBLANES), :]
                                     for sl in range(n_slab)], axis=-1)
            out_ref[rows, :] = x_ref[rows, :] + y_run


def _ffn_call(layer, x, gpre, wu, cw, cb, wd, gpost, seq):
    tokens = x.shape[0]
    n_chunks = D_FF // FF_CHUNK
    halo_per_tile = TM_FFN // HALO
    vec = pl.BlockSpec((None, 1, D_MODEL), lambda i, c: (layer, 0, 0))
    return pl.pallas_call(
        functools.partial(_ffn_body, tiles_per_seq=seq // TM_FFN),
        grid=(tokens // TM_FFN, n_chunks),
        in_specs=[pl.BlockSpec((TM_FFN, D_MODEL), lambda i, c: (i, 0)),
                  pl.BlockSpec((HALO, D_MODEL), lambda i, c: (jnp.maximum(i * halo_per_tile - 1, 0), 0)),
                  vec,
                  pl.BlockSpec((None, D_MODEL, FF_CHUNK), lambda i, c: (layer, 0, c)),
                  pl.BlockSpec((None, D_MODEL, FF_CHUNK), lambda i, c: (layer, 0, n_chunks + c)),
                  pl.BlockSpec((None, 3, FF_CHUNK), lambda i, c: (layer, 0, c)),
                  pl.BlockSpec((None, 3, FF_CHUNK), lambda i, c: (layer, 0, n_chunks + c)),
                  pl.BlockSpec((None, 1, FF_CHUNK), lambda i, c: (layer, 0, c)),
                  pl.BlockSpec((None, 1, FF_CHUNK), lambda i, c: (layer, 0, n_chunks + c)),
                  pl.BlockSpec((None, FF_CHUNK, D_MODEL), lambda i, c: (layer, c, 0)),
                  vec],
        out_specs=pl.BlockSpec((TM_FFN, D_MODEL), lambda i, c: (i, 0)),
        out_shape=jax.ShapeDtypeStruct((tokens, D_MODEL), F32),
        scratch_shapes=[pltpu.VMEM((HALO + TM_FFN, D_MODEL), BF16), pltpu.VMEM((TM_FFN, D_MODEL), F32),
                        pltpu.VMEM((TM_FFN, FF_CHUNK), BF16),
                        pltpu.VMEM((max(D_MODEL, 4 * FF_SUB) // LANES, HALO + TM_FFN, LANES), F32)],
        compiler_params=_params(2),
        name="conv_ffn",
    )(x, x, gpre, wu, wu, cw, cw, cb, cb, wd, gpost)


def _rope_tables(positions):
    inv_freq = ROPE_THETA ** (-jnp.arange(0, ROT_DIM, 2, dtype=F32) / ROT_DIM)
    lane = jnp.arange(LANES, dtype=jnp.int32) % HEAD_DIM
    ang = positions.astype(F32).reshape(-1, 1) * inv_freq[lane % ROT_HALF][None, :]
    cos = jnp.where(lane < ROT_DIM, jnp.cos(ang), 1.0)
    sin = jnp.sin(ang)
    return (cos, jnp.where(lane < ROT_HALF, -sin, 0.0),
            jnp.where((lane >= ROT_HALF) & (lane < ROT_DIM), sin, 0.0))


def kernel(x, positions, norm_pre_mix, w_in, b_gate, w_proj_a, w_proj_b, w_out, norm_post_mix,
           norm_pre_ffn, w_up, conv_w, conv_b, w_down, norm_post_ffn):
    batch, seq, _ = x.shape
    depth = w_in.shape[0]
    assert seq % TM == 0 and seq % TM_FFN == 0 and TM % MOBA_BLOCK == 0 and seq // DIL_PAIRS[-1][1] == BAND
    cos, sin_lo, sin_hi = _rope_tables(positions)
    w_i = w_in.astype(BF16)
    w_pa, w_pb, w_o = w_proj_a.astype(BF16), w_proj_b.astype(BF16), w_out.astype(BF16)
    w_u, w_d = w_up.astype(BF16), w_down.astype(BF16)
    rows = lambda p: p.reshape(depth, 1, -1)
    g_pre_mix, g_post_mix = rows(norm_pre_mix), rows(norm_post_mix)
    g_pre_ffn, g_post_ffn = rows(norm_pre_ffn), rows(norm_post_ffn)
    bias_gate, bias_conv = rows(b_gate), rows(conv_b)

    xt = x.reshape(batch * seq, D_MODEL)
    for l in range(depth):
        a0, a1, a2, bqkv, kmean = _qkv_call(l, xt, g_pre_mix, w_i, cos, sin_lo, sin_hi, batch, seq)
        ya = _dilated_call(a0, a1, a2, batch, seq)
        yb = _moba_call(bqkv, kmean.reshape(batch, seq // MOBA_BLOCK, WIDTH_B), batch, seq)
        xt = _mix_call(l, xt, ya, yb, g_pre_mix, w_i, bias_gate, w_pa, w_pb, w_o, g_post_mix)
        xt = _ffn_call(l, xt, g_pre_ffn, w_u, conv_w, bias_conv, w_d, g_post_ffn, seq)
    return xt.reshape(batch, seq, D_MODEL)
```
